```python
import math
import jax
import jax.numpy as jnp
from jax import lax
import numpy as np

D_MODEL = 1024
BATCH = 2
SEQ = 8192
DEPTH = 4
DEC_BATCH = 32
DEC_SEQ = 4
PAST_LEN = 8192
PAGE_SIZE = 128

N_MIXERS = 3
N_LAYERS_A = (DEPTH + 2) // 3
N_LAYERS_B = (DEPTH + 1) // 3
N_LAYERS_C = DEPTH // 3

A_HEADS = 16
A_HEAD_DIM = D_MODEL // A_HEADS
A_KV_GROUPS = 4
A_GROUP_SIZE = A_HEADS // A_KV_GROUPS
A_WIDTH = A_HEADS * A_HEAD_DIM
A_KV_WIDTH = 2 * A_KV_GROUPS * A_HEAD_DIM
A_IN = 2 * A_WIDTH + 3 * A_KV_WIDTH + 3 * A_HEADS
CMP_STRIDE = 16
CMP_LEN = 2 * CMP_STRIDE
SLC_BLOCK = 64
CMP_PER_SLC = SLC_BLOCK // CMP_STRIDE
N_SELECT = 16
WINDOW = 512
Q_BLOCK = 128

DN_HEADS = 8
DN_HEAD_DIM = D_MODEL // DN_HEADS
DN_WIDTH = DN_HEADS * DN_HEAD_DIM
DN_CONV = 4
DN_CHUNK = 64
DN_IN = 4 * DN_WIDTH + 2 * DN_HEADS

CONF_WIDTH = D_MODEL
CONF_KERNEL = 31
CONF_IN = 3 * CONF_WIDTH

NORM_EPS = 1e-6
NEG_INF = -1e30
FORCE_BONUS = 1e4
TINY = 1e-30

kernel_name = 'hybrid_nsa_gdn_conformer_step'


def rms_norm(x, g):
    xf = x.astype(jnp.float32)
    y = xf * lax.rsqrt(jnp.mean(xf * xf, axis=-1, keepdims=True) + NORM_EPS)
    return (y * g.astype(jnp.float32)).astype(x.dtype)


def layer_norm(x, g, b):
    xf = x.astype(jnp.float32)
    mu = jnp.mean(xf, axis=-1, keepdims=True)
    var = jnp.mean(jnp.square(xf - mu), axis=-1, keepdims=True)
    y = (xf - mu) * lax.rsqrt(var + NORM_EPS)
    return (y * g.astype(jnp.float32) + b.astype(jnp.float32)).astype(x.dtype)


def l2_norm(x):
    return x * lax.rsqrt(jnp.sum(x * x, axis=-1, keepdims=True) + NORM_EPS)


def alibi_slopes(n_heads):
    return jnp.exp2(-8.0 * jnp.arange(1, n_heads + 1, dtype=jnp.float32) / n_heads)


def masked_probs(logits, valid):
    logits = jnp.where(valid, logits, NEG_INF)
    m = jnp.max(logits, axis=-1, keepdims=True)
    p = jnp.where(valid, jnp.exp(logits - m), 0.0)
    return p / (jnp.sum(p, axis=-1, keepdims=True) + TINY)


def causal_dwconv(x_ext, w):
    return lax.conv_general_dilated(x_ext, w[:, None, :].astype(x_ext.dtype), window_strides=(1,),
                                    padding='VALID', dimension_numbers=('NWC', 'WIO', 'NWC'),
                                    feature_group_count=x_ext.shape[-1])


def gather_pages(pool, page_table):
    rows = pool[page_table]
    return rows.reshape((page_table.shape[0], -1) + pool.shape[2:])


def key_norm_rows(kv, g):
    return jnp.stack([rms_norm(kv[:, :, 0], g), kv[:, :, 1]], axis=2)


def compress_kv(rows, pos_emb, w_cmp, k_gain):
    B, L = rows.shape[0], rows.shape[1]
    nh = L // CMP_STRIDE
    halves = rows[:, :nh * CMP_STRIDE].reshape(B, nh, CMP_STRIDE, 2, A_KV_GROUPS, A_HEAD_DIM)
    w = w_cmp.reshape(2, 2, CMP_STRIDE, A_HEAD_DIM, A_HEAD_DIM)
    pe = pos_emb.reshape(2, 2, CMP_STRIDE, A_HEAD_DIM)
    proj = jnp.einsum('bnlcgd,chlde->bnhcge', halves, w)
    bias = jnp.einsum('chld,chlde->ce', pe, w)
    comp = proj[:, :-1, 0] + proj[:, 1:, 1] + bias[:, None, :]
    kc = rms_norm(comp[:, :, 0], k_gain)
    vc = comp[:, :, 1]
    cpos = jnp.arange(nh - 1, dtype=jnp.int32) * CMP_STRIDE + (CMP_LEN - 1)
    return kc, vc, cpos


def nsa_mixer(h, w_in, q_gain, k_gain, cmp_pos, cmp_w, gate_b, w_out, slopes, q_pos0,
              past_cmp, past_slc, win_buf):
    B, T, _ = h.shape
    G, R, dh = A_KV_GROUPS, A_GROUP_SIZE, A_HEAD_DIM
    p = h @ w_in
    q, kv_c, kv_s, kv_w, gl, z = jnp.split(
        p, [A_WIDTH, A_WIDTH + A_KV_WIDTH, A_WIDTH + 2 * A_KV_WIDTH, A_WIDTH + 3 * A_KV_WIDTH,
            A_WIDTH + 3 * A_KV_WIDTH + 3 * A_HEADS], axis=-1)
    q = rms_norm(q.reshape(B, T, A_HEADS, dh), q_gain) * (dh ** -0.5)
    q = q.reshape(B, T, G, R, dh)
    kv_c = kv_c.reshape(B, T, 2, G, dh)
    kv_s = key_norm_rows(kv_s.reshape(B, T, 2, G, dh), k_gain[1])
    kv_w = key_norm_rows(kv_w.reshape(B, T, 2, G, dh), k_gain[2])
    if past_cmp is None:
        full_c, full_s, full_w = kv_c, kv_s, kv_w
        w_off = 0
        new_win = kv_w[:, T - min(WINDOW, T):]
    else:
        full_c = jnp.concatenate([past_cmp.astype(kv_c.dtype), kv_c], axis=1)
        full_s = jnp.concatenate([past_slc.astype(kv_s.dtype), kv_s], axis=1)
        full_w = jnp.concatenate([win_buf.astype(kv_w.dtype), kv_w], axis=1)
        w_buf = win_buf.shape[1]
        w_off = q_pos0 - w_buf
        new_win = full_w[:, full_w.shape[1] - w_buf:]

    kc, vc, cpos = compress_kv(full_c, cmp_pos, cmp_w, k_gain[0])
    nc = kc.shape[1]
    L = full_s.shape[1]
    n_sel = -(-L // SLC_BLOCK)
    s_pad = jnp.pad(full_s, ((0, 0), (0, n_sel * SLC_BLOCK - L), (0, 0), (0, 0), (0, 0)))
    s_blk = s_pad.reshape(B, n_sel, SLC_BLOCK, 2, G, dh).transpose(3, 0, 4, 1, 2, 5)
    ks_blk, vs_blk = s_blk[0], s_blk[1]
    n_top = min(N_SELECT, n_sel)
    blk_ids = jnp.arange(n_sel, dtype=jnp.int32)

    QB = min(Q_BLOCK, T)
    nqb = -(-T // QB)
    Tp = nqb * QB
    w_pad = jnp.pad(full_w, ((0, 0), (WINDOW, QB), (0, 0), (0, 0), (0, 0)))
    q_blocks = jnp.pad(q, ((0, 0), (0, Tp - T), (0, 0), (0, 0), (0, 0)))
    q_blocks = q_blocks.reshape(B, nqb, QB, G, R, dh).transpose(1, 0, 2, 3, 4, 5)
    starts = q_pos0 + jnp.arange(nqb, dtype=jnp.int32) * QB
    sl = slopes.reshape(G, R)[None, :, :, None, None]
    gather = jax.vmap(jax.vmap(lambda blocks, ix: blocks[ix]))

    def attend_block(args):
        qb, s = args
        t = s + jnp.arange(QB, dtype=jnp.int32)
        dist_c = (t[:, None] - cpos[None, :]).astype(jnp.float32)
        lc = jnp.einsum('bqgrd,bngd->bgrqn', qb, kc).astype(jnp.float32) - sl * dist_c
        pc = masked_probs(lc, cpos[None, :] <= t[:, None])
        o_c = jnp.einsum('bgrqn,bngd->bqgrd', pc.astype(vc.dtype), vc)
        imp = jnp.pad(pc.sum(axis=2), ((0, 0), (0, 0), (0, 0), (0, n_sel * CMP_PER_SLC - nc)))
        imp = imp.reshape(B, G, QB, n_sel, CMP_PER_SLC).sum(-1)
        jt = (t // SLC_BLOCK)[:, None]
        forced = (blk_ids == 0) | (blk_ids == jt) | (blk_ids == jt - 1)
        score = jnp.where(blk_ids > jt, NEG_INF, imp + jnp.where(forced, FORCE_BONUS, 0.0))
        _, idx = lax.top_k(score, n_top)
        ksel = gather(ks_blk, idx).reshape(B, G, QB, n_top * SLC_BLOCK, dh)
        vsel = gather(vs_blk, idx).reshape(B, G, QB, n_top * SLC_BLOCK, dh)
        kpos = (idx[..., None] * SLC_BLOCK + jnp.arange(SLC_BLOCK, dtype=jnp.int32)).reshape(B, G, QB, n_top * SLC_BLOCK)
        dist_s = (t[None, None, None, :, None] - kpos[:, :, None]).astype(jnp.float32)
        ls = jnp.einsum('bqgrd,bgqmd->bgrqm', qb, ksel).astype(jnp.float32) - sl * dist_s
        ps = masked_probs(ls, (kpos <= t[None, None, :, None])[:, :, None])
        o_s = jnp.einsum('bgrqm,bgqmd->bqgrd', ps.astype(vsel.dtype), vsel)
        band = lax.dynamic_slice_in_dim(w_pad, s - w_off, WINDOW + QB, axis=1)
        wpos = s - WINDOW + jnp.arange(WINDOW + QB, dtype=jnp.int32)
        dist_w = t[:, None] - wpos[None, :]
        valid_w = (dist_w >= 0) & (dist_w < WINDOW) & (wpos[None, :] >= w_off)
        lw = jnp.einsum('bqgrd,bkgd->bgrqk', qb, band[:, :, 0]).astype(jnp.float32) - sl * dist_w.astype(jnp.float32)
        pw = masked_probs(lw, valid_w)
        o_w = jnp.einsum('bgrqk,bkgd->bqgrd', pw.astype(band.dtype), band[:, :, 1])
        return o_c, o_s, o_w

    o_c, o_s, o_w = lax.map(attend_block, (q_blocks, starts))
    unblock = lambda o: o.transpose(1, 0, 2, 3, 4, 5).reshape(B, Tp, G, R, dh)[:, :T]
    gates = jax.nn.sigmoid(gl + gate_b).reshape(B, T, 3, G, R, 1)
    o = gates[:, :, 0] * unblock(o_c) + gates[:, :, 1] * unblock(o_s) + gates[:, :, 2] * unblock(o_w)
    out = (o.reshape(B, T, A_WIDTH) * jax.nn.silu(z)) @ w_out
    return out, kv_c, kv_s, new_win


def chunk_gated_delta(q, k, v, g, beta, s0):
    B, T, H, dk = q.shape
    dv = v.shape[-1]
    C = min(DN_CHUNK, T)
    n = -(-T // C)
    pad = n * C - T

    def blocks(a):
        a = jnp.pad(a, [(0, 0), (0, pad)] + [(0, 0)] * (a.ndim - 2))
        a = a.reshape((B, n, C) + a.shape[2:])
        return jnp.moveaxis(jnp.moveaxis(a, 1, 0), 3, 2)

    qc, kc, vc, gc, bc = blocks(q), blocks(k), blocks(v), blocks(g), blocks(beta)
    Gc = jnp.cumsum(gc, axis=-1)
    pos = jnp.arange(C)
    tril = pos[:, None] >= pos[None, :]
    stril = pos[:, None] > pos[None, :]
    diff = Gc[..., :, None] - Gc[..., None, :]
    decay = jnp.where(tril, jnp.exp(jnp.where(tril, diff, 0.0)), 0.0)
    kb = kc * bc[..., None]
    a_mat = jnp.where(stril, jnp.einsum('nbhid,nbhjd->nbhij', kb, kc) * decay, 0.0)
    eye = jnp.eye(C, dtype=jnp.float32)
    t_mat = lax.linalg.triangular_solve(eye + a_mat, jnp.broadcast_to(eye, a_mat.shape),
                                        left_side=True, lower=True, unit_diagonal=True)
    u = t_mat @ (vc * bc[..., None])
    w = t_mat @ (kb * jnp.exp(Gc)[..., None])
    qk = jnp.where(tril, jnp.einsum('nbhid,nbhjd->nbhij', qc, kc) * decay, 0.0)

    def step(s, xs):
        q_i, k_i, u_i, w_i, qk_i, g_i = xs
        v_new = u_i - w_i @ s
        o_i = (q_i * jnp.exp(g_i)[..., None]) @ s + qk_i @ v_new
        g_last = g_i[..., -1:]
        s = s * jnp.exp(g_last)[..., None] + jnp.einsum('bhck,bhcv->bhkv', k_i * jnp.exp(g_last - g_i)[..., None], v_new)
        return s, o_i

    s_final, o = lax.scan(step, s0, (qc, kc, u, w, qk, Gc))
    o = jnp.moveaxis(jnp.moveaxis(o, 2, 3), 0, 1).reshape(B, n * C, H, dv)[:, :T]
    return o, s_final


def gdn_mixer(h, w_in, conv_w, a_log, dt_bias, o_gain, w_out, conv_buf, s0):
    B, T, _ = h.shape
    p = h @ w_in
    qkv, z, b_logit, a_logit = jnp.split(p, [3 * DN_WIDTH, 4 * DN_WIDTH, 4 * DN_WIDTH + DN_HEADS], axis=-1)
    x_ext = jnp.concatenate([conv_buf.astype(qkv.dtype), qkv], axis=1)
    qkv = jax.nn.silu(causal_dwconv(x_ext, conv_w)).reshape(B, T, 3, DN_HEADS, DN_HEAD_DIM).astype(jnp.float32)
    q = l2_norm(qkv[:, :, 0]) * (DN_HEAD_DIM ** -0.5)
    k = l2_norm(qkv[:, :, 1])
    v = qkv[:, :, 2]
    beta = jax.nn.sigmoid(b_logit.astype(jnp.float32))
    g = -jnp.exp(a_log.astype(jnp.float32)) * jax.nn.softplus(a_logit.astype(jnp.float32) + dt_bias.astype(jnp.float32))
    o, s_new = chunk_gated_delta(q, k, v, g, beta, s0.astype(jnp.float32))
    o = rms_norm(o, o_gain).astype(h.dtype).reshape(B, T, DN_WIDTH)
    out = (o * jax.nn.silu(z)) @ w_out
    return out, s_new, x_ext[:, x_ext.shape[1] - (DN_CONV - 1):]


def conformer_mixer(h, w_in, conv_w, conv_b, ln_g, ln_b, w_out, conv_buf):
    p = h @ w_in
    a, b, z = jnp.split(p, 3, axis=-1)
    u = a * jax.nn.sigmoid(b)
    x_ext = jnp.concatenate([conv_buf.astype(u.dtype), u], axis=1)
    c = causal_dwconv(x_ext, conv_w) + conv_b
    c = jax.nn.silu(layer_norm(c, ln_g, ln_b))
    out = (c * jax.nn.silu(z)) @ w_out
    return out, x_ext[:, x_ext.shape[1] - (CONF_KERNEL - 1):]


def setup_inputs(seed: int = 0) -> dict:
    key = jax.random.key(seed)
    keys = iter(jax.random.split(key, 40))

    def normal(shape, scale):
        return jax.random.normal(next(keys), shape, jnp.float32) * scale

    def gain(shape):
        return 1.0 + normal(shape, 0.05)

    n_pages = PAST_LEN // PAGE_SIZE
    n_pool = (DEC_BATCH * n_pages * 5) // 4
    w_buf = min(WINDOW, PAST_LEN)
    page_table = jax.random.permutation(next(keys), n_pool)[:DEC_BATCH * n_pages].reshape(DEC_BATCH, n_pages).astype(jnp.int32)
    dt = jnp.exp(jax.random.uniform(next(keys), (N_LAYERS_B, DN_HEADS), jnp.float32, math.log(1e-3), math.log(1e-1)))
    pool_shape = (N_LAYERS_A, n_pool, PAGE_SIZE, 2, A_KV_GROUPS, A_HEAD_DIM)
    return {
        'x_prompt': normal((BATCH, SEQ, D_MODEL), 1.0),
        'x_sample': normal((DEC_BATCH, DEC_SEQ, D_MODEL), 1.0),
        'cache_cmp_kv': normal(pool_shape, 1.0),
        'cache_slc_kv': normal(pool_shape, 1.0),
        'cache_win_kv': normal((N_LAYERS_A, DEC_BATCH, w_buf, 2, A_KV_GROUPS, A_HEAD_DIM), 1.0),
        'state_delta': normal((N_LAYERS_B, DEC_BATCH, DN_HEADS, DN_HEAD_DIM, DN_HEAD_DIM), 0.1),
        'state_delta_conv': normal((N_LAYERS_B, DEC_BATCH, DN_CONV - 1, 3 * DN_WIDTH), 1.0),
        'state_conv': normal((N_LAYERS_C, DEC_BATCH, CONF_KERNEL - 1, CONF_WIDTH), 1.0),
        'page_table': page_table,
        'norm_g': gain((DEPTH, D_MODEL)),
        'a_w_in': normal((N_LAYERS_A, D_MODEL, A_IN), D_MODEL ** -0.5),
        'a_q_gain': gain((N_LAYERS_A, A_HEAD_DIM)),
        'a_k_gain': gain((N_LAYERS_A, 3, A_HEAD_DIM)),
        'a_cmp_pos': normal((N_LAYERS_A, 2, CMP_LEN, A_HEAD_DIM), 0.1),
        'a_cmp_w': normal((N_LAYERS_A, 2, CMP_LEN * A_HEAD_DIM, A_HEAD_DIM), (CMP_LEN * A_HEAD_DIM) ** -0.5),
        'a_gate_b': normal((N_LAYERS_A, 3 * A_HEADS), 0.1),
        'a_w_out': normal((N_LAYERS_A, A_WIDTH, D_MODEL), A_WIDTH ** -0.5),
        'b_w_in': normal((N_LAYERS_B, D_MODEL, DN_IN), D_MODEL ** -0.5),
        'b_conv_w': normal((N_LAYERS_B, DN_CONV, 3 * DN_WIDTH), DN_CONV ** -0.5),
        'b_a_log': jnp.log(jax.random.uniform(next(keys), (N_LAYERS_B, DN_HEADS), jnp.float32, 1.0, 16.0)),
        'b_dt_bias': jnp.log(jnp.expm1(dt)),
        'b_o_gain': gain((N_LAYERS_B, DN_HEAD_DIM)),
        'b_w_out': normal((N_LAYERS_B, DN_WIDTH, D_MODEL), DN_WIDTH ** -0.5),
        'c_w_in': normal((N_LAYERS_C, D_MODEL, CONF_IN), D_MODEL ** -0.5),
        'c_conv_w': normal((N_LAYERS_C, CONF_KERNEL, CONF_WIDTH), CONF_KERNEL ** -0.5),
        'c_conv_b': normal((N_LAYERS_C, CONF_WIDTH), 0.02),
        'c_ln_g': gain((N_LAYERS_C, CONF_WIDTH)),
        'c_ln_b': normal((N_LAYERS_C, CONF_WIDTH), 0.02),
        'c_w_out': normal((N_LAYERS_C, CONF_WIDTH, D_MODEL), CONF_WIDTH ** -0.5),
    }


def reference(x_prompt, x_sample, cache_cmp_kv, cache_slc_kv, cache_win_kv, state_delta, state_delta_conv,
              state_conv, page_table, norm_g, a_w_in, a_q_gain, a_k_gain, a_cmp_pos, a_cmp_w, a_gate_b, a_w_out,
              b_w_in, b_conv_w, b_a_log, b_dt_bias, b_o_gain, b_w_out, c_w_in, c_conv_w, c_conv_b, c_ln_g, c_ln_b,
              c_w_out):
    slopes = alibi_slopes(A_HEADS)
    yp, ys = x_prompt, x_sample
    bp = x_prompt.shape[0]
    cmp_p, cmp_s, slc_p, slc_s, win_p, win_s = [], [], [], [], [], []
    dst_p, dst_s, dcv_p, dcv_s, ccv_p, ccv_s = [], [], [], [], [], []
    for i in range(DEPTH):
        j = i // N_MIXERS
        kind = i % N_MIXERS
        hp = rms_norm(yp, norm_g[i])
        hs = rms_norm(ys, norm_g[i])
        if kind == 0:
            wa = (a_w_in[j], a_q_gain[j], a_k_gain[j], a_cmp_pos[j], a_cmp_w[j], a_gate_b[j], a_w_out[j], slopes)
            op, c_rows_p, s_rows_p, w_rows_p = nsa_mixer(hp, *wa, 0, None, None, None)
            os_, c_rows_s, s_rows_s, w_rows_s = nsa_mixer(
                hs, *wa, PAST_LEN, gather_pages(cache_cmp_kv[j], page_table),
                gather_pages(cache_slc_kv[j], page_table), cache_win_kv[j])
            cmp_p.append(c_rows_p)
            cmp_s.append(c_rows_s)
            slc_p.append(s_rows_p)
            slc_s.append(s_rows_s)
            win_p.append(w_rows_p)
            win_s.append(w_rows_s)
        elif kind == 1:
            wb = (b_w_in[j], b_conv_w[j], b_a_log[j], b_dt_bias[j], b_o_gain[j], b_w_out[j])
            op, st_p, cv_p = gdn_mixer(hp, *wb, jnp.zeros((bp, DN_CONV - 1, 3 * DN_WIDTH), hp.dtype),
                                       jnp.zeros((bp, DN_HEADS, DN_HEAD_DIM, DN_HEAD_DIM), jnp.float32))
            os_, st_s, cv_s = gdn_mixer(hs, *wb, state_delta_conv[j], state_delta[j])
            dst_p.append(st_p)
            dst_s.append(st_s)
            dcv_p.append(cv_p)
            dcv_s.append(cv_s)
        else:
            wc = (c_w_in[j], c_conv_w[j], c_conv_b[j], c_ln_g[j], c_ln_b[j], c_w_out[j])
            op, cb_p = conformer_mixer(hp, *wc, jnp.zeros((bp, CONF_KERNEL - 1, CONF_WIDTH), hp.dtype))
            os_, cb_s = conformer_mixer(hs, *wc, state_conv[j])
            ccv_p.append(cb_p)
            ccv_s.append(cb_s)
        yp = yp + op
        ys = ys + os_
    cmp_kv_prompt, cmp_kv_sample = jnp.stack(cmp_p), jnp.stack(cmp_s)
    slc_kv_prompt, slc_kv_sample = jnp.stack(slc_p), jnp.stack(slc_s)
    win_kv_prompt, win_kv_sample = jnp.stack(win_p), jnp.stack(win_s)
    delta_state_prompt, delta_state_sample = jnp.stack(dst_p), jnp.stack(dst_s)
    delta_conv_prompt, delta_conv_sample = jnp.stack(dcv_p), jnp.stack(dcv_s)
    conv_prompt, conv_sample = jnp.stack(ccv_p), jnp.stack(ccv_s)
    return (yp, ys, cmp_kv_prompt, cmp_kv_sample, slc_kv_prompt, slc_kv_sample, win_kv_prompt, win_kv_sample,
            delta_state_prompt, delta_state_sample, delta_conv_prompt, delta_conv_sample, conv_prompt, conv_sample)
```

```python
import functools
import math

import jax
import jax.numpy as jnp
from jax import lax
from jax.experimental import pallas as pl
from jax.experimental.pallas import tpu as pltpu

F32 = jnp.float32
BF16 = jnp.bfloat16
HI = lax.Precision.HIGHEST

D_MODEL = 1024
N_MIXERS = 3
A_HEADS = 16
A_HEAD_DIM = 64
A_KV_GROUPS = 4
A_GROUP_SIZE = 4
A_KV_WIDTH = 2 * A_KV_GROUPS * A_HEAD_DIM
CMP_STRIDE = 16
CMP_LEN = 32
SLC_BLOCK = 64
N_SELECT = 16
WINDOW = 512
PAGE_SIZE = 128
N_CMP = 512
N_BLK = 128
DN_HEADS = 8
DN_HEAD_DIM = 128
DN_WIDTH = 1024
DN_CONV = 4
DN_CHUNK = 64
CONF_KERNEL = 31

NORM_EPS = 1e-6
NEG_INF = -1e30
FORCE_BONUS = 1e4
TINY = 1e-30

LANE = 128
SUBLANE = 8
VMEM_LIMIT = 56 * 1024 * 1024


def _cp(*sem):
    return pltpu.CompilerParams(dimension_semantics=sem, vmem_limit_bytes=VMEM_LIMIT)


def _dot(a, b):
    return jnp.dot(a, b, preferred_element_type=F32)


def _dot_nt(a, b, precision=None):
    return lax.dot_general(a, b, (((1,), (1,)), ((), ())), precision=precision,
                           preferred_element_type=F32)


def _dot_tn(a, b, precision=None):
    return lax.dot_general(a, b, (((0,), (0,)), ((), ())), precision=precision,
                           preferred_element_type=F32)


def _sigmoid(x):
    return 1.0 / (1.0 + jnp.exp(-x))


def _silu(x):
    return x * _sigmoid(x)


def _rms_rows(x, g):
    return x * lax.rsqrt(jnp.mean(x * x, axis=-1, keepdims=True) + NORM_EPS) * g


def _seg_norm(v, bd, gain):
    ss = _dot((v * v).astype(BF16), bd)
    return v * lax.rsqrt(ss * (1.0 / A_HEAD_DIM) + NORM_EPS) * gain


def _const_spec(shape):
    nd = len(shape)
    return pl.BlockSpec(shape, lambda *_: (0,) * nd)


def _nsa_proj_body(x_ref, g_ref, w_ref, bd_ref, qg_ref, ksg_ref, kwg_ref, gb_ref,
                   q_ref, kvc_ref, kvs_ref, kvw_ref, sz_ref, gt_ref):
    h = _rms_rows(x_ref[...], g_ref[...])
    p = _dot(h.astype(BF16), w_ref[...])
    bd = bd_ref[...]
    bd4 = bd[:256, :256]
    q_ref[...] = _seg_norm(p[:, :1024], bd, qg_ref[...]).astype(BF16)
    kvc_ref[...] = p[:, 1024:1536]
    kvs_ref[:, :256] = _seg_norm(p[:, 1536:1792], bd4, ksg_ref[...])
    kvs_ref[:, 256:] = p[:, 1792:2048]
    kvw_ref[:, :256] = _seg_norm(p[:, 2048:2304], bd4, kwg_ref[...])
    kvw_ref[:, 256:] = p[:, 2304:2560]
    sz_ref[...] = _silu(p[:, 2560:3584])
    gt_ref[...] = _sigmoid(p[:, 3584:3712] + gb_ref[...])


def _nsa_proj(x, g, w, bd, qg, ksg, kwg, gb):
    m = x.shape[0]
    tm = min(256, m)
    row = lambda n: pl.BlockSpec((tm, n), lambda i: (i, 0))
    return pl.pallas_call(
        _nsa_proj_body,
        grid=(m // tm,),
        in_specs=[row(1024), _const_spec((1, 1024)), _const_spec(w.shape), _const_spec(bd.shape),
                  _const_spec((1, 1024)), _const_spec((1, 256)), _const_spec((1, 256)),
                  _const_spec((1, 128))],
        out_specs=[row(1024), row(512), row(512), row(512), row(1024), row(128)],
        out_shape=[jax.ShapeDtypeStruct((m, 1024), BF16), jax.ShapeDtypeStruct((m, 512), F32),
                   jax.ShapeDtypeStruct((m, 512), F32), jax.ShapeDtypeStruct((m, 512), F32),
                   jax.ShapeDtypeStruct((m, 1024), F32), jax.ShapeDtypeStruct((m, 128), F32)],
        compiler_params=_cp("parallel"),
        name="nsa_proj",
    )(x, g, w, bd, qg, ksg, kwg, gb)


def _gdn_proj_body(x_ref, g_ref, w_ref, qkv_ref, sz_ref, ba_ref):
    h = _rms_rows(x_ref[...], g_ref[...])
    p = _dot(h.astype(BF16), w_ref[...])
    qkv_ref[...] = p[:, :3072]
    sz_ref[...] = _silu(p[:, 3072:4096])
    ba_ref[...] = p[:, 4096:4224]


def _gdn_proj(x, g, w):
    m = x.shape[0]
    tm = min(256, m)
    row = lambda n: pl.BlockSpec((tm, n), lambda i: (i, 0))
    return pl.pallas_call(
        _gdn_proj_body,
        grid=(m // tm,),
        in_specs=[row(1024), _const_spec((1, 1024)), _const_spec(w.shape)],
        out_specs=[row(3072), row(1024), row(128)],
        out_shape=[jax.ShapeDtypeStruct((m, 3072), F32), jax.ShapeDtypeStruct((m, 1024), F32),
                   jax.ShapeDtypeStruct((m, 128), F32)],
        compiler_params=_cp("parallel"),
        name="gdn_proj",
    )(x, g, w)


def _conf_proj_body(x_ref, g_ref, w_ref, u_ref, sz_ref):
    h = _rms_rows(x_ref[...], g_ref[...])
    p = _dot(h.astype(BF16), w_ref[...])
    u_ref[...] = p[:, :1024] * _sigmoid(p[:, 1024:2048])
    sz_ref[...] = _silu(p[:, 2048:3072])


def _conf_proj(x, g, w):
    m = x.shape[0]
    tm = min(256, m)
    row = lambda n: pl.BlockSpec((tm, n), lambda i: (i, 0))
    return pl.pallas_call(
        _conf_proj_body,
        grid=(m // tm,),
        in_specs=[row(1024), _const_spec((1, 1024)), _const_spec(w.shape)],
        out_specs=[row(1024), row(1024)],
        out_shape=[jax.ShapeDtypeStruct((m, 1024), F32), jax.ShapeDtypeStruct((m, 1024), F32)],
        compiler_params=_cp("parallel"),
        name="conf_proj",
    )(x, g, w)


def _out_body(x_ref, a_ref, sz_ref, w_ref, y_ref):
    a = (a_ref[...] * sz_ref[...]).astype(BF16)
    y_ref[...] = x_ref[...] + _dot(a, w_ref[...])


def _out_proj(x, a, sz, w):
    m = x.shape[0]
    tm = min(256, m)
    row = pl.BlockSpec((tm, 1024), lambda i: (i, 0))
    return pl.pallas_call(
        _out_body,
        grid=(m // tm,),
        in_specs=[row, row, row, _const_spec(w.shape)],
        out_specs=row,
        out_shape=jax.ShapeDtypeStruct((m, 1024), F32),
        compiler_params=_cp("parallel"),
        name="out_proj",
    )(x, a, sz, w)


def _nsa_out_body(x_ref, oc_ref, os_ref, ow_ref, gt_ref, sz_ref, ex_ref, w_ref, y_ref):
    gt = gt_ref[...]
    hi = gt.astype(BF16)
    lo = (gt - hi.astype(F32)).astype(BF16)
    ge = _dot(hi, ex_ref[...]) + _dot(lo, ex_ref[...])
    o = ge[:, :1024] * oc_ref[...] + ge[:, 1024:2048] * os_ref[...] + ge[:, 2048:] * ow_ref[...]
    a = (o * sz_ref[...]).astype(BF16)
    y_ref[...] = x_ref[...] + _dot(a, w_ref[...])


def _nsa_out(x, oc, os_, ow, gt, sz, ex, w):
    m = x.shape[0]
    tm = min(256, m)
    row = lambda n: pl.BlockSpec((tm, n), lambda i: (i, 0))
    return pl.pallas_call(
        _nsa_out_body,
        grid=(m // tm,),
        in_specs=[row(1024), row(1024), row(1024), row(1024), row(128), row(1024),
                  _const_spec(ex.shape), _const_spec(w.shape)],
        out_specs=row(1024),
        out_shape=jax.ShapeDtypeStruct((m, 1024), F32),
        compiler_params=_cp("parallel"),
        name="nsa_out",
    )(x, oc, os_, ow, gt, sz, ex, w)


CMP_PAGES = 32


def _cmp_proj_body(tbl_ref, *refs):
    page_refs = refs[:CMP_PAGES]
    w_ref, pe_ref, proj_ref, bias_ref, xs = refs[CMP_PAGES:]
    rows = CMP_PAGES * 8
    for k in range(CMP_PAGES):
        xs[k * 8:(k + 1) * 8, :] = page_refs[k][0]
    xs[rows:rows + 8, :] = pe_ref[...]
    acc = jnp.zeros((rows + 8, 1024), F32)
    for l in range(CMP_STRIDE):
        acc = acc + _dot(xs[:, l * 512:(l + 1) * 512].astype(BF16), w_ref[l * 512:(l + 1) * 512, :])
    proj_ref[0] = acc[:rows]
    bias_ref[...] = acc[rows:]


def _cmp_proj(pool, table, wbig, pe):
    nseq, npg = table.shape
    steps = npg // CMP_PAGES

    def page_spec(k):
        return pl.BlockSpec((1, 8, 8192), lambda s, j, tbl: (tbl[s, j * CMP_PAGES + k], 0, 0))

    grid_spec = pltpu.PrefetchScalarGridSpec(
        num_scalar_prefetch=1,
        grid=(nseq, steps),
        in_specs=[page_spec(k) for k in range(CMP_PAGES)] + [
            pl.BlockSpec((8192, 1024), lambda s, j, tbl: (0, 0), pipeline_mode=pl.Buffered(1)),
            pl.BlockSpec((8, 8192), lambda s, j, tbl: (0, 0))],
        out_specs=[pl.BlockSpec((1, CMP_PAGES * 8, 1024), lambda s, j, tbl: (s, j, 0)),
                   pl.BlockSpec((8, 1024), lambda s, j, tbl: (0, 0))],
        scratch_shapes=[pltpu.VMEM((CMP_PAGES * 8 + 8, 8192), F32)],
    )
    return pl.pallas_call(
        _cmp_proj_body,
        grid_spec=grid_spec,
        out_shape=[jax.ShapeDtypeStruct((nseq, N_CMP, 1024), F32),
                   jax.ShapeDtypeStruct((8, 1024), F32)],
        compiler_params=_cp("arbitrary", "arbitrary"),
        name="cmp_proj",
    )(table, *([pool] * CMP_PAGES), wbig, pe)


def _cmp_fin_body(proj_ref, bias_ref, bd_ref, kg_ref, out_ref, tmp):
    acc = proj_ref[0]
    nxt = pltpu.roll(acc[:, 512:], N_CMP - 1, axis=0)
    bias = bias_ref[0:1, 0:512] + bias_ref[1:2, 512:1024]
    comp = acc[:, :512] + nxt + bias
    kn = _seg_norm(comp[:, :256], bd_ref[...], kg_ref[...])
    for g in range(A_KV_GROUPS):
        tmp[g] = jnp.concatenate(
            [kn[:, g * 64:(g + 1) * 64], comp[:, 256 + g * 64:256 + (g + 1) * 64]], axis=-1)
        for j in range(4):
            out_ref[0, g, j * N_BLK:(j + 1) * N_BLK, :] = tmp[g, pl.ds(j, N_BLK, stride=4), :]


def _cmp_fin(proj, bias, bd4, kg):
    nseq = proj.shape[0]
    return pl.pallas_call(
        _cmp_fin_body,
        grid=(nseq,),
        in_specs=[pl.BlockSpec((1, N_CMP, 1024), lambda s: (s, 0, 0)), _const_spec((8, 1024)),
                  _const_spec((256, 256)), _const_spec((1, 256))],
        out_specs=pl.BlockSpec((1, A_KV_GROUPS, N_CMP, 128), lambda s: (s, 0, 0, 0)),
        out_shape=jax.ShapeDtypeStruct((nseq, A_KV_GROUPS, N_CMP, 128), F32),
        scratch_shapes=[pltpu.VMEM((A_KV_GROUPS, N_CMP, 128), F32)],
        compiler_params=_cp("parallel"),
        name="cmp_fin",
    )(proj, bias, bd4, kg)


def _slopes(g, shape, axis):
    r = lax.broadcasted_iota(jnp.int32, shape, axis)
    return jnp.exp2(-0.5 * (g * A_GROUP_SIZE + r + 1).astype(F32))


def _cmp_attn_body(q_ref, kcv_ref, oc_ref, sel_ref, *, qb, t0, n_pick):
    g = pl.program_id(1)
    i = pl.program_id(2)
    R = A_GROUP_SIZE
    q = q_ref[0, 0].reshape(R * qb, 128)
    t = t0 + i * qb + lax.broadcasted_iota(jnp.int32, (1, qb, 1), 1)
    blk = lax.broadcasted_iota(jnp.int32, (1, 1, N_BLK), 2)
    slope = _slopes(g, (R, 1, 1), 0)
    logits, valids, kvs = [], [], []
    mx = jnp.full((R, qb, 1), NEG_INF, F32)
    for j in range(4):
        kv = kcv_ref[0, 0, j * N_BLK:(j + 1) * N_BLK, :].astype(BF16)
        cpos = SLC_BLOCK * blk + (CMP_STRIDE * j + CMP_LEN - 1)
        valid = cpos <= t
        lg = _dot_nt(q, kv).reshape(R, qb, N_BLK) - slope * (t - cpos).astype(F32)
        lg = jnp.where(valid, lg, NEG_INF)
        mx = jnp.maximum(mx, jnp.max(lg, axis=-1, keepdims=True))
        logits.append(lg)
        valids.append(valid)
        kvs.append(kv)
    ps = [jnp.where(v, jnp.exp(lg - mx), 0.0) for lg, v in zip(logits, valids)]
    den = sum(jnp.sum(p, axis=-1, keepdims=True) for p in ps) + TINY
    acc = jnp.zeros((R * qb, 128), F32)
    imp = jnp.zeros((qb, N_BLK), F32)
    for j in range(4):
        pc = ps[j] / den
        acc = acc + _dot(pc.reshape(R * qb, N_BLK).astype(BF16), kvs[j])
        imp = imp + jnp.sum(pc, axis=0)
    acc = acc.reshape(R, qb, 128)
    oc_ref[0] = jnp.concatenate([acc[r, :, 64:] for r in range(R)], axis=-1)

    t2 = t[0]
    blk2 = blk[0]
    blkf = blk2.astype(F32)
    jt = t2 // SLC_BLOCK
    forced = (blk2 == 0) | (blk2 == jt) | (blk2 == jt - 1)
    score = jnp.where(blk2 > jt, NEG_INF, imp + jnp.where(forced, FORCE_BONUS, 0.0))
    sel = jnp.zeros((qb, N_BLK), F32)
    for _ in range(n_pick):
        smax = jnp.max(score, axis=-1, keepdims=True)
        first = jnp.min(jnp.where(score == smax, blkf, float(N_BLK)), axis=-1, keepdims=True)
        pick = blkf == first
        sel = jnp.where(pick, 1.0, sel)
        score = jnp.where(pick, -jnp.inf, score)
    sel_ref[0, 0] = sel


def _cmp_attn(qe, kcv, *, qb, t0, n_pick):
    ns, G, R, T, _ = qe.shape
    nqb = T // qb
    return pl.pallas_call(
        functools.partial(_cmp_attn_body, qb=qb, t0=t0, n_pick=n_pick),
        grid=(ns, G, nqb),
        in_specs=[pl.BlockSpec((1, 1, R, qb, 128), lambda b, g, i: (b, g, 0, i, 0)),
                  pl.BlockSpec((1, 1, N_CMP, 128), lambda b, g, i: (b, g, 0, 0))],
        out_specs=[pl.BlockSpec((1, qb, 256), lambda b, g, i: (b, i, g)),
                   pl.BlockSpec((1, 1, qb, N_BLK), lambda b, g, i: (b, g, i, 0))],
        out_shape=[jax.ShapeDtypeStruct((ns, T, 1024), F32),
                   jax.ShapeDtypeStruct((ns, G, T, N_BLK), F32)],
        compiler_params=_cp("parallel", "parallel", "parallel"),
        name="cmp_attn",
    )(qe, kcv)


SLC_TK = 256
WIN_TK = 128
QB = 128


def _flash_step(q, kv, valid, bias, carry):
    m, l, acc = carry
    R, qb = m.shape[0], m.shape[1]
    tk = kv.shape[0]
    s = _dot_nt(q, kv).reshape(R, qb, tk) - bias
    s = jnp.where(valid, s, NEG_INF)
    mn = jnp.maximum(m, jnp.max(s, axis=-1, keepdims=True))
    alpha = jnp.exp(m - mn)
    p = jnp.where(valid, jnp.exp(s - mn), 0.0)
    l = alpha * l + jnp.sum(p, axis=-1, keepdims=True)
    pv = _dot(p.reshape(R * qb, tk).astype(BF16), kv).reshape(R, qb, 128)
    return mn, l, alpha * acc + pv


def _flash_init(R, qb):
    return (jnp.full((R, qb, 1), NEG_INF, F32), jnp.zeros((R, qb, 1), F32),
            jnp.zeros((R, qb, 128), F32))


def _flash_out(carry):
    _, l, acc = carry
    o = acc / (l + TINY)
    return jnp.concatenate([o[r, :, 64:] for r in range(o.shape[0])], axis=-1)


def _p_attn_body(q_ref, sel_ref, kvs_ref, kvw_ref, os_ref, ow_ref):
    g = pl.program_id(1)
    i = pl.program_id(2)
    R = A_GROUP_SIZE
    q = q_ref[0, 0].reshape(R * QB, 128)
    t = i * QB + lax.broadcasted_iota(jnp.int32, (1, QB, 1), 1)
    slope = _slopes(g, (R, 1, 1), 0)
    selb = sel_ref[0, 0].astype(BF16)

    def slc_step(kt, carry):
        k0 = pl.multiple_of(kt * SLC_TK, SLC_TK)
        kv = kvs_ref[0, 0, pl.ds(k0, SLC_TK), :]
        kpos2 = k0 + lax.broadcasted_iota(jnp.int32, (N_BLK, SLC_TK), 1)
        blk2 = lax.broadcasted_iota(jnp.int32, (N_BLK, SLC_TK), 0)
        expand = jnp.where(kpos2 // SLC_BLOCK == blk2, 1.0, 0.0).astype(BF16)
        chosen = _dot(selb, expand)
        kpos = k0 + lax.broadcasted_iota(jnp.int32, (1, 1, SLC_TK), 2)
        valid = (chosen[None] > 0.5) & (kpos <= t)
        bias = slope * (t - kpos).astype(F32)
        return _flash_step(q, kv, valid, bias, carry)

    n_slc = (i * QB + QB + SLC_TK - 1) // SLC_TK
    os_ref[0] = _flash_out(lax.fori_loop(0, n_slc, slc_step, _flash_init(R, QB)))

    def win_step(kt, carry):
        k0 = pl.multiple_of(kt * WIN_TK, WIN_TK)
        kv = kvw_ref[0, 0, pl.ds(k0, WIN_TK), :]
        kpos = k0 + lax.broadcasted_iota(jnp.int32, (1, 1, WIN_TK), 2)
        dist = t - kpos
        valid = (dist >= 0) & (dist < WINDOW)
        return _flash_step(q, kv, valid, slope * dist.astype(F32), carry)

    lo = jnp.maximum(i - WINDOW // WIN_TK, 0)
    ow_ref[0] = _flash_out(lax.fori_loop(lo, i + 1, win_step, _flash_init(R, QB)))


def _p_attn(qe, sel, kvs, kvw):
    B, G, R, T, _ = qe.shape
    kv_spec = pl.BlockSpec((1, 1, T, 128), lambda b, g, i: (b, g, 0, 0))
    o_spec = pl.BlockSpec((1, QB, 256), lambda b, g, i: (b, i, g))
    return pl.pallas_call(
        _p_attn_body,
        grid=(B, G, T // QB),
        in_specs=[pl.BlockSpec((1, 1, R, QB, 128), lambda b, g, i: (b, g, 0, i, 0)),
                  pl.BlockSpec((1, 1, QB, N_BLK), lambda b, g, i: (b, g, i, 0)),
                  kv_spec, kv_spec],
        out_specs=[o_spec, o_spec],
        out_shape=[jax.ShapeDtypeStruct((B, T, 1024), F32)] * 2,
        compiler_params=_cp("parallel", "parallel", "arbitrary"),
        name="p_attn",
    )(qe, sel, kvs, kvw)


S_PAGES = 8
SQ = 8
S_ROWS = A_HEADS * SQ


def _s_rows_meta(past_len):
    row = lax.broadcasted_iota(jnp.int32, (S_ROWS, 1), 0)
    slope = jnp.exp2(-0.5 * (row // SQ + 1).astype(F32))
    t = past_len + row % SQ
    return slope, t


def _s_step(q, kv, valid, bias, m, l, acc):
    s = _dot_nt(q, kv) - bias
    s = jnp.where(valid, s, NEG_INF)
    mn = jnp.maximum(m, jnp.max(s, axis=-1, keepdims=True))
    alpha = jnp.exp(m - mn)
    p = jnp.where(valid, jnp.exp(s - mn), 0.0)
    l = alpha * l + jnp.sum(p, axis=-1, keepdims=True)
    return mn, l, alpha * acc + _dot(p.astype(BF16), kv)


def _s_out(l, acc):
    o = acc / (l + TINY)
    pieces = []
    for g in range(A_KV_GROUPS):
        for r in range(A_GROUP_SIZE):
            r0 = (g * A_GROUP_SIZE + r) * SQ
            pieces.append(o[r0:r0 + SQ, 256 + g * 64:256 + (g + 1) * 64])
    return jnp.concatenate(pieces, axis=-1)


def _s_attn_body(tbl_ref, *refs, past_len):
    page_refs = refs[:S_PAGES]
    q_ref, sel_ref, new_s_ref, win_ref, new_w_ref, os_ref, ow_ref, m_sc, l_sc, acc_sc = refs[S_PAGES:]
    j = pl.program_id(1)
    nj = pl.num_programs(1)
    q = q_ref[0]
    slope, t = _s_rows_meta(past_len)
    sel = sel_ref[0]
    selr = jnp.broadcast_to(sel[:, None], (A_KV_GROUPS, A_GROUP_SIZE, SQ, N_BLK))
    selr = selr.reshape(S_ROWS, N_BLK).astype(BF16)

    @pl.when(j == 0)
    def _():
        m_sc[...] = jnp.full((S_ROWS, 1), NEG_INF, F32)
        l_sc[...] = jnp.zeros((S_ROWS, 1), F32)
        acc_sc[...] = jnp.zeros((S_ROWS, 512), F32)

    m, l, acc = m_sc[...], l_sc[...], acc_sc[...]
    for k in range(S_PAGES):
        kv = page_refs[k][0].astype(BF16)
        k0 = (j * S_PAGES + k) * PAGE_SIZE
        kpos2 = k0 + lax.broadcasted_iota(jnp.int32, (N_BLK, PAGE_SIZE), 1)
        blk2 = lax.broadcasted_iota(jnp.int32, (N_BLK, PAGE_SIZE), 0)
        expand = jnp.where(kpos2 // SLC_BLOCK == blk2, 1.0, 0.0).astype(BF16)
        chosen = _dot(selr, expand)
        kpos = k0 + lax.broadcasted_iota(jnp.int32, (1, PAGE_SIZE), 1)
        valid = (chosen > 0.5) & (kpos <= t)
        m, l, acc = _s_step(q, kv, valid, slope * (t - kpos).astype(F32), m, l, acc)
    m_sc[...], l_sc[...], acc_sc[...] = m, l, acc

    @pl.when(j == nj - 1)
    def _():
        knew = past_len + lax.broadcasted_iota(jnp.int32, (1, PAGE_SIZE), 1)
        dnew = t - knew
        kv = new_s_ref[0].astype(BF16)
        _, l2, acc2 = _s_step(q, kv, dnew >= 0, slope * dnew.astype(F32), m, l, acc)
        os_ref[0] = _s_out(l2, acc2)
        nbuf = win_ref.shape[1]
        wpos = (past_len - nbuf) + lax.broadcasted_iota(jnp.int32, (1, nbuf), 1)
        dw = t - wpos
        mw = jnp.full((S_ROWS, 1), NEG_INF, F32)
        lw = jnp.zeros((S_ROWS, 1), F32)
        aw = jnp.zeros((S_ROWS, 512), F32)
        mw, lw, aw = _s_step(q, win_ref[0].astype(BF16), (dw >= 0) & (dw < WINDOW),
                             slope * dw.astype(F32), mw, lw, aw)
        _, lw, aw = _s_step(q, new_w_ref[0].astype(BF16), (dnew >= 0) & (dnew < WINDOW),
                            slope * dnew.astype(F32), mw, lw, aw)
        ow_ref[0] = _s_out(lw, aw)


def _s_attn(table, pool, qx, sel, new_s, win, new_w, *, past_len):
    ns, npg = table.shape
    steps = npg // S_PAGES

    def page_spec(k):
        return pl.BlockSpec((1, PAGE_SIZE, 512), lambda s, j, tbl: (tbl[s, j * S_PAGES + k], 0, 0))

    per_seq = lambda shape: pl.BlockSpec((1,) + shape, lambda s, j, tbl: (s,) + (0,) * len(shape))
    grid_spec = pltpu.PrefetchScalarGridSpec(
        num_scalar_prefetch=1,
        grid=(ns, steps),
        in_specs=[page_spec(k) for k in range(S_PAGES)] + [
            per_seq((S_ROWS, 512)), per_seq((A_KV_GROUPS, SQ, N_BLK)), per_seq((PAGE_SIZE, 512)),
            per_seq((win.shape[1], 512)), per_seq((PAGE_SIZE, 512))],
        out_specs=[per_seq((SQ, 1024)), per_seq((SQ, 1024))],
        scratch_shapes=[pltpu.VMEM((S_ROWS, 1), F32), pltpu.VMEM((S_ROWS, 1), F32),
                        pltpu.VMEM((S_ROWS, 512), F32)],
    )
    return pl.pallas_call(
        functools.partial(_s_attn_body, past_len=past_len),
        grid_spec=grid_spec,
        out_shape=[jax.ShapeDtypeStruct((ns, SQ, 1024), F32)] * 2,
        compiler_params=_cp("parallel", "arbitrary"),
        name="s_attn",
    )(table, *([pool] * S_PAGES), qx, sel, new_s, win, new_w)


def _gdn_body(qkv_ref, ba_ref, h0_ref, s0_ref, cw_ref, alog_ref, dt_ref, og_ref,
              o_ref, st_ref, buf, S, *, C, t_valid):
    i = pl.program_id(1)
    n_i = pl.num_programs(1)

    @pl.when(i == 0)
    def _():
        buf[0:8, :] = h0_ref[0]
        S[...] = s0_ref[0]

    buf[8:8 + C, :] = qkv_ref[0]
    y = jnp.zeros((C, 3 * DN_WIDTH), F32)
    for j in range(DN_CONV):
        y = y + cw_ref[j:j + 1, :] * buf[pl.ds(8 - (DN_CONV - 1) + j, C), :]
    y = _silu(y)
    if C >= 8:
        buf[0:8, :] = buf[C:C + 8, :]

    ba = ba_ref[0]
    sp_in = ba + dt_ref[...]
    softplus = jnp.maximum(sp_in, 0.0) + jnp.log(1.0 + jnp.exp(-jnp.abs(sp_in)))
    g_all = -jnp.exp(alog_ref[...]) * softplus
    beta_all = _sigmoid(ba)
    row = lax.broadcasted_iota(jnp.int32, (C, C), 0)
    col = lax.broadcasted_iota(jnp.int32, (C, C), 1)
    if t_valid < C:
        live = lax.broadcasted_iota(jnp.int32, (C, 1), 0) < t_valid
        g_all = jnp.where(live, g_all, 0.0)
        beta_all = jnp.where(live, beta_all, 0.0)
    tril = row >= col
    stril = row > col
    eye = jnp.where(row == col, 1.0, 0.0)
    ones = jnp.ones((C, C), F32)
    gc_all = jnp.dot(jnp.where(tril, 1.0, 0.0), g_all, precision=HI, preferred_element_type=F32)

    outs = []
    for h in range(DN_HEADS):
        qh = y[:, h * 128:(h + 1) * 128]
        kh = y[:, DN_WIDTH + h * 128:DN_WIDTH + (h + 1) * 128]
        vh = y[:, 2 * DN_WIDTH + h * 128:2 * DN_WIDTH + (h + 1) * 128]
        qh = qh * lax.rsqrt(jnp.sum(qh * qh, axis=-1, keepdims=True) + NORM_EPS) * (DN_HEAD_DIM ** -0.5)
        kh = kh * lax.rsqrt(jnp.sum(kh * kh, axis=-1, keepdims=True) + NORM_EPS)
        bh = beta_all[:, h:h + 1]
        gc = gc_all[:, 8 + h:9 + h]
        eg = jnp.exp(gc)
        gc_row = jnp.dot(ones, eye * gc, precision=HI, preferred_element_type=F32)
        decay = jnp.where(tril, jnp.exp(jnp.where(tril, gc - gc_row, 0.0)), 0.0)
        kb = kh * bh
        a_mat = jnp.where(stril, _dot_nt(kb, kh, HI) * decay, 0.0)
        pw = -a_mat
        t_mat = eye + pw
        for _ in range(max(int(math.log2(C)) - 1, 0)):
            pw = jnp.dot(pw, pw, precision=HI, preferred_element_type=F32)
            t_mat = t_mat + jnp.dot(t_mat, pw, precision=HI, preferred_element_type=F32)
        u = jnp.dot(t_mat, vh * bh, precision=HI, preferred_element_type=F32)
        w = jnp.dot(t_mat, kb * eg, precision=HI, preferred_element_type=F32)
        qk = jnp.where(tril, _dot_nt(qh, kh, HI) * decay, 0.0)
        s_h = S[h]
        v_new = u - jnp.dot(w, s_h, precision=HI, preferred_element_type=F32)
        o_h = (jnp.dot(qh * eg, s_h, precision=HI, preferred_element_type=F32)
               + jnp.dot(qk, v_new, precision=HI, preferred_element_type=F32))
        g_last = gc[C - 1:C, :]
        S[h] = s_h * jnp.exp(g_last) + _dot_tn(kh * jnp.exp(g_last - gc), v_new, HI)
        outs.append(_rms_rows(o_h, og_ref[...]))
    o_ref[0] = jnp.concatenate(outs, axis=-1)

    @pl.when(i == n_i - 1)
    def _():
        st_ref[0] = S[...]


def _gdn_core(qkv, ba, h0, s0, cw, alog, dt, og, *, C, t_valid):
    B, T, _ = qkv.shape
    per_b = lambda shape: pl.BlockSpec((1,) + shape, lambda b, i: (b,) + (0,) * len(shape))
    tile = lambda n: pl.BlockSpec((1, C, n), lambda b, i: (b, i, 0))
    return pl.pallas_call(
        functools.partial(_gdn_body, C=C, t_valid=t_valid),
        grid=(B, T // C),
        in_specs=[tile(3 * DN_WIDTH), tile(128), per_b((8, 3 * DN_WIDTH)),
                  per_b((DN_HEADS, 128, 128)), _const_spec((8, 3 * DN_WIDTH)),
                  _const_spec((1, 128)), _const_spec((1, 128)), _const_spec((1, 128))],
        out_specs=[tile(DN_WIDTH), per_b((DN_HEADS, 128, 128))],
        out_shape=[jax.ShapeDtypeStruct((B, T, DN_WIDTH), F32),
                   jax.ShapeDtypeStruct((B, DN_HEADS, 128, 128), F32)],
        scratch_shapes=[pltpu.VMEM((8 + C, 3 * DN_WIDTH), F32),
                        pltpu.VMEM((DN_HEADS, 128, 128), F32)],
        compiler_params=_cp("parallel", "arbitrary"),
        name="gdn_core",
    )(qkv, ba, h0, s0, cw, alog, dt, og)


CONF_HIST = 32


def _conf_body(u_ref, h0_ref, cw_ref, cb_ref, lg_ref, lb_ref, c_ref, buf, *, tm, carry):
    i = pl.program_id(1)

    @pl.when(i == 0)
    def _():
        buf[0:CONF_HIST, :] = h0_ref[0]

    buf[CONF_HIST:CONF_HIST + tm, :] = u_ref[0]
    acc = jnp.zeros((tm, 1024), F32)
    for j in range(CONF_KERNEL):
        acc = acc + cw_ref[j:j + 1, :] * buf[pl.ds(CONF_HIST - (CONF_KERNEL - 1) + j, tm), :]
    c = acc + cb_ref[...]
    mu = jnp.mean(c, axis=-1, keepdims=True)
    var = jnp.mean(jnp.square(c - mu), axis=-1, keepdims=True)
    c = (c - mu) * lax.rsqrt(var + NORM_EPS) * lg_ref[...] + lb_ref[...]
    c_ref[0] = _silu(c)
    if carry:
        buf[0:CONF_HIST, :] = buf[tm:tm + CONF_HIST, :]


def _conf_conv(u, h0, cw, cb, lg, lb):
    B, T, _ = u.shape
    tm = min(256, T)
    return pl.pallas_call(
        functools.partial(_conf_body, tm=tm, carry=T > tm),
        grid=(B, T // tm),
        in_specs=[pl.BlockSpec((1, tm, 1024), lambda b, i: (b, i, 0)),
                  pl.BlockSpec((1, CONF_HIST, 1024), lambda b, i: (b, 0, 0)),
                  _const_spec((32, 1024)), _const_spec((1, 1024)), _const_spec((1, 1024)),
                  _const_spec((1, 1024))],
        out_specs=pl.BlockSpec((1, tm, 1024), lambda b, i: (b, i, 0)),
        out_shape=jax.ShapeDtypeStruct((B, T, 1024), F32),
        scratch_shapes=[pltpu.VMEM((CONF_HIST + tm, 1024), F32)],
        compiler_params=_cp("parallel", "arbitrary"),
        name="conf_conv",
    )(u, h0, cw, cb, lg, lb)


def _block_diag_ones(n, seg):
    r = jnp.arange(n) // seg
    return (r[:, None] == r[None, :]).astype(BF16)


def _nsa_weights(w_in, q_gain, k_gain, cmp_pos, cmp_w, gate_b, w_out):
    q_w, kvc_w, kvs_w, kvw_w, gl_w, z_w = jnp.split(
        w_in, [1024, 1536, 2048, 2560, 2608], axis=1)
    wcat = jnp.concatenate([q_w, kvc_w, kvs_w, kvw_w, z_w, jnp.pad(gl_w, ((0, 0), (0, 80)))],
                           axis=1).astype(BF16)
    w5 = cmp_w.reshape(2, 2, CMP_STRIDE, A_HEAD_DIM, A_HEAD_DIM)
    eye_c = jnp.eye(2, dtype=F32)
    eye_g = jnp.eye(A_KV_GROUPS, dtype=F32)
    wbig = jnp.einsum('chlde,ca,gb->labdhcge', w5, eye_c, eye_g).reshape(8192, 1024).astype(BF16)
    pe4 = cmp_pos.reshape(2, 2, CMP_STRIDE, A_HEAD_DIM)
    pe = jnp.broadcast_to(pe4.transpose(1, 2, 0, 3)[:, :, :, None, :],
                          (2, CMP_STRIDE, 2, A_KV_GROUPS, A_HEAD_DIM)).reshape(2, 8192)
    pe = jnp.pad(pe, ((0, 6), (0, 0)))
    heads = jnp.arange(1024) // A_HEAD_DIM
    ex = jnp.concatenate(
        [(jnp.arange(128)[:, None] == (br * A_HEADS + heads)[None, :]) for br in range(3)],
        axis=1).astype(BF16)
    return dict(
        wcat=wcat, wbig=wbig, pe=pe, ex=ex,
        bd=_block_diag_ones(1024, A_HEAD_DIM),
        qg=(jnp.tile(q_gain, A_HEADS) * (A_HEAD_DIM ** -0.5))[None, :],
        kcg=jnp.tile(k_gain[0], A_KV_GROUPS)[None, :],
        ksg=jnp.tile(k_gain[1], A_KV_GROUPS)[None, :],
        kwg=jnp.tile(k_gain[2], A_KV_GROUPS)[None, :],
        gb=jnp.pad(gate_b, (0, 80))[None, :],
        w_out=w_out.astype(BF16))


def _pack_kv(kv, B, T):
    kv = kv.reshape(B, T, 2, A_KV_GROUPS, A_HEAD_DIM).transpose(0, 3, 1, 2, 4)
    return kv.reshape(B, A_KV_GROUPS, T, 128).astype(BF16)


def _nsa_layer(y, g_norm, wts, B, T, *, past=None):
    M = B * T
    x2 = y.reshape(M, D_MODEL)
    qn, kvc, kvs, kvw, sz, gt = _nsa_proj(x2, g_norm, wts['wcat'], wts['bd'], wts['qg'],
                                           wts['ksg'], wts['kwg'], wts['gb'])
    bd4 = wts['bd'][:256, :256]
    q5 = qn.reshape(B, T, A_KV_GROUPS, A_GROUP_SIZE, A_HEAD_DIM).transpose(0, 2, 3, 1, 4)
    if past is None:
        n_pg = T // PAGE_SIZE
        pool = kvc.reshape(B * n_pg, 8, 8192)
        table = jnp.arange(B * n_pg, dtype=jnp.int32).reshape(B, n_pg)
        proj, bias = _cmp_proj(pool, table, wts['wbig'], wts['pe'])
        kcv = _cmp_fin(proj, bias, bd4, wts['kcg'])
        qe = jnp.pad(q5, ((0, 0),) * 4 + ((0, 64),))
        oc, sel = _cmp_attn(qe, kcv, qb=QB, t0=0, n_pick=N_SELECT)
        os_, ow = _p_attn(qe, sel, _pack_kv(kvs, B, T), _pack_kv(kvw, B, T))
        oc, os_, ow = (o.reshape(M, 1024) for o in (oc, os_, ow))
        new_win = kvw.reshape(B, T, A_KV_WIDTH)[:, T - min(WINDOW, T):]
    else:
        cmp_pool, slc_pool, win_buf, table, past_len = past
        n_pool = cmp_pool.shape[0]
        proj, bias = _cmp_proj(cmp_pool.reshape(n_pool, 8, 8192), table, wts['wbig'], wts['pe'])
        kcv = _cmp_fin(proj, bias, bd4, wts['kcg'])
        q5p = jnp.pad(q5, ((0, 0), (0, 0), (0, 0), (0, SQ - T), (0, 0)))
        qe = jnp.pad(q5p, ((0, 0),) * 4 + ((0, 64),))
        oc, sel = _cmp_attn(qe, kcv, qb=SQ, t0=past_len, n_pick=N_SELECT - 1)
        eye_g = jnp.eye(A_KV_GROUPS, dtype=BF16)
        qx = jnp.einsum('bgrqd,gh->bgrqhd', q5p, eye_g).reshape(B, S_ROWS, 256)
        qx = jnp.pad(qx, ((0, 0), (0, 0), (0, 256)))
        pad_new = lambda a: jnp.pad(a.reshape(B, T, A_KV_WIDTH), ((0, 0), (0, PAGE_SIZE - T), (0, 0)))
        w_rows = win_buf.reshape(B, -1, A_KV_WIDTH)
        os_, ow = _s_attn(table, slc_pool.reshape(n_pool, PAGE_SIZE, A_KV_WIDTH), qx, sel,
                          pad_new(kvs), w_rows, pad_new(kvw), past_len=past_len)
        oc, os_, ow = (o[:, :T].reshape(M, 1024) for o in (oc, os_, ow))
        full_w = jnp.concatenate([w_rows, kvw.reshape(B, T, A_KV_WIDTH)], axis=1)
        new_win = full_w[:, full_w.shape[1] - w_rows.shape[1]:]
    y_new = _nsa_out(x2, oc, os_, ow, gt, sz, wts['ex'], wts['w_out']).reshape(B, T, D_MODEL)
    shape6 = lambda a: a.reshape(B, -1, 2, A_KV_GROUPS, A_HEAD_DIM)
    return y_new, shape6(kvc), shape6(kvs), shape6(new_win)


def _gdn_layer(y, g_norm, w_in, conv_w, a_log, dt_bias, o_gain, w_out, B, T, conv_buf, s0):
    M = B * T
    x2 = y.reshape(M, D_MODEL)
    wcat = jnp.pad(w_in, ((0, 0), (0, 112))).astype(BF16)
    qkv, sz, ba = _gdn_proj(x2, g_norm, wcat)
    qkv3 = qkv.reshape(B, T, 3 * DN_WIDTH)
    C = DN_CHUNK if T >= DN_CHUNK else 8
    Tp = -(-T // C) * C
    pad_t = lambda a: jnp.pad(a, ((0, 0), (0, Tp - T), (0, 0)))
    h0 = jnp.pad(conv_buf, ((0, 0), (8 - (DN_CONV - 1), 0), (0, 0)))
    lane8 = lambda v: jnp.pad(v, (8, 112))[None, :]
    o, s_new = _gdn_core(pad_t(qkv3), pad_t(ba.reshape(B, T, 128)), h0, s0,
                         jnp.pad(conv_w, ((0, 8 - DN_CONV), (0, 0))), lane8(a_log), lane8(dt_bias),
                         o_gain[None, :], C=C, t_valid=min(T, C) if Tp != T else C)
    y_new = _out_proj(x2, o[:, :T].reshape(M, DN_WIDTH), sz, w_out.astype(BF16))
    x_ext_tail = jnp.concatenate([conv_buf, qkv3], axis=1)[:, T:]
    return y_new.reshape(B, T, D_MODEL), s_new, x_ext_tail


def _conf_layer(y, g_norm, w_in, conv_w, conv_b, ln_g, ln_b, w_out, B, T, conv_buf):
    M = B * T
    x2 = y.reshape(M, D_MODEL)
    u, sz = _conf_proj(x2, g_norm, w_in.astype(BF16))
    u3 = u.reshape(B, T, 1024)
    h0 = jnp.pad(conv_buf, ((0, 0), (CONF_HIST - (CONF_KERNEL - 1), 0), (0, 0)))
    c = _conf_conv(u3, h0, jnp.pad(conv_w, ((0, 32 - CONF_KERNEL), (0, 0))), conv_b[None, :],
                   ln_g[None, :], ln_b[None, :])
    y_new = _out_proj(x2, c.reshape(M, 1024), sz, w_out.astype(BF16))
    x_ext_tail = jnp.concatenate([conv_buf, u3], axis=1)[:, T:]
    return y_new.reshape(B, T, D_MODEL), x_ext_tail


def kernel(x_prompt, x_sample, cache_cmp_kv, cache_slc_kv, cache_win_kv, state_delta, state_delta_conv, state_conv, page_table, norm_g, a_w_in, a_q_gain, a_k_gain, a_cmp_pos, a_cmp_w, a_gate_b, a_w_out, b_w_in, b_conv_w, b_a_log, b_dt_bias, b_o_gain, b_w_out, c_w_in, c_conv_w, c_conv_b, c_ln_g, c_ln_b, c_w_out):
    depth = norm_g.shape[0]
    bp, tp, _ = x_prompt.shape
    bs, ts, _ = x_sample.shape
    past_len = page_table.shape[1] * PAGE_SIZE
    yp, ys = x_prompt, x_sample
    outs = [[] for _ in range(12)]
    for i in range(depth):
        j, kind = divmod(i, N_MIXERS)
        gn = norm_g[i][None, :]
        if kind == 0:
            wts = _nsa_weights(a_w_in[j], a_q_gain[j], a_k_gain[j], a_cmp_pos[j], a_cmp_w[j],
                               a_gate_b[j], a_w_out[j])
            yp, cp_, sp_, wp_ = _nsa_layer(yp, gn, wts, bp, tp)
            ys, cs_, ss_, ws_ = _nsa_layer(
                ys, gn, wts, bs, ts,
                past=(cache_cmp_kv[j], cache_slc_kv[j], cache_win_kv[j], page_table, past_len))
            for lst, v in zip(outs[0:6], (cp_, cs_, sp_, ss_, wp_, ws_)):
                lst.append(v)
        elif kind == 1:
            wb = (b_w_in[j], b_conv_w[j], b_a_log[j], b_dt_bias[j], b_o_gain[j], b_w_out[j])
            yp, st_p, cv_p = _gdn_layer(yp, gn, *wb, bp, tp,
                                        jnp.zeros((bp, DN_CONV - 1, 3 * DN_WIDTH), F32),
                                        jnp.zeros((bp, DN_HEADS, DN_HEAD_DIM, DN_HEAD_DIM), F32))
            ys, st_s, cv_s = _gdn_layer(ys, gn, *wb, bs, ts, state_delta_conv[j], state_delta[j])
            for lst, v in zip(outs[6:10], (st_p, st_s, cv_p, cv_s)):
                lst.append(v)
        else:
            wc = (c_w_in[j], c_conv_w[j], c_conv_b[j], c_ln_g[j], c_ln_b[j], c_w_out[j])
            yp, cb_p = _conf_layer(yp, gn, *wc, bp, tp, jnp.zeros((bp, CONF_KERNEL - 1, 1024), F32))
            ys, cb_s = _conf_layer(ys, gn, *wc, bs, ts, state_conv[j])
            for lst, v in zip(outs[10:12], (cb_p, cb_s)):
                lst.append(v)
    return (yp, ys) + tuple(jnp.stack(lst) for lst in outs)
```

```python
import functools
import math

import jax
import jax.numpy as jnp
from jax import lax
from jax.experimental import pallas as pl
from jax.experimental.pallas import tpu as pltpu

F32 = jnp.float32
BF16 = jnp.bfloat16
HI = lax.Precision.HIGHEST

D_MODEL = 1024
N_MIXERS = 3
A_HEADS = 16
A_HEAD_DIM = 64
A_KV_GROUPS = 4
A_GROUP_SIZE = 4
A_KV_WIDTH = 2 * A_KV_GROUPS * A_HEAD_DIM
CMP_STRIDE = 16
CMP_LEN = 32
SLC_BLOCK = 64
N_SELECT = 16
WINDOW = 512
PAGE_SIZE = 128
N_CMP = 512
N_BLK = 128
DN_HEADS = 8
DN_HEAD_DIM = 128
DN_WIDTH = 1024
DN_CONV = 4
DN_CHUNK = 64
CONF_KERNEL = 31

NORM_EPS = 1e-6
NEG_INF = -1e30
FORCE_BONUS = 1e4
TINY = 1e-30
MASK_BIG = 2.0 ** 40

LANE = 128
SUBLANE = 8
VMEM_LIMIT = 56 * 1024 * 1024


def _cp(*sem):
    return pltpu.CompilerParams(dimension_semantics=sem, vmem_limit_bytes=VMEM_LIMIT)


def _dot(a, b):
    return jnp.dot(a, b, preferred_element_type=F32)


def _dot_nt(a, b, precision=None):
    return lax.dot_general(a, b, (((1,), (1,)), ((), ())), precision=precision,
                           preferred_element_type=F32)


def _dot_tn(a, b, precision=None):
    return lax.dot_general(a, b, (((0,), (0,)), ((), ())), precision=precision,
                           preferred_element_type=F32)


def _dot3(a, b):
    a_hi = a.astype(BF16)
    b_hi = b.astype(BF16)
    a_lo = (a - a_hi.astype(F32)).astype(BF16)
    b_lo = (b - b_hi.astype(F32)).astype(BF16)
    return _dot(a_hi, b_hi) + _dot(a_hi, b_lo) + _dot(a_lo, b_hi)


def _sigmoid(x):
    return 1.0 / (1.0 + jnp.exp(-x))


def _silu(x):
    return x * _sigmoid(x)


def _rms_rows(x, g):
    return x * lax.rsqrt(jnp.mean(x * x, axis=-1, keepdims=True) + NORM_EPS) * g


def _seg_norm(v, bd, gain):
    ss = _dot((v * v).astype(BF16), bd)
    return v * lax.rsqrt(ss * (1.0 / A_HEAD_DIM) + NORM_EPS) * gain


def _const_spec(shape):
    nd = len(shape)
    return pl.BlockSpec(shape, lambda *_: (0,) * nd)


def _nsa_proj_body(x_ref, g_ref, w_ref, bd_ref, qg_ref, ksg_ref, kwg_ref, gb_ref,
                   q_ref, kvc_ref, kvs_ref, kvw_ref, sz_ref, gt_ref):
    h = _rms_rows(x_ref[...], g_ref[...])
    p = _dot(h.astype(BF16), w_ref[...])
    bd = bd_ref[...]
    bd4 = bd[:256, :256]
    q_ref[...] = _seg_norm(p[:, :1024], bd, qg_ref[...]).astype(BF16)
    kvc_ref[...] = p[:, 1024:1536]
    kvs_ref[:, :256] = _seg_norm(p[:, 1536:1792], bd4, ksg_ref[...])
    kvs_ref[:, 256:] = p[:, 1792:2048]
    kvw_ref[:, :256] = _seg_norm(p[:, 2048:2304], bd4, kwg_ref[...])
    kvw_ref[:, 256:] = p[:, 2304:2560]
    sz_ref[...] = _silu(p[:, 2560:3584])
    gt_ref[...] = _sigmoid(p[:, 3584:3712] + gb_ref[...])


def _nsa_proj(x, g, w, bd, qg, ksg, kwg, gb):
    m = x.shape[0]
    tm = min(256, m)
    row = lambda n: pl.BlockSpec((tm, n), lambda i: (i, 0))
    return pl.pallas_call(
        _nsa_proj_body,
        grid=(m // tm,),
        in_specs=[row(1024), _const_spec((1, 1024)), _const_spec(w.shape), _const_spec(bd.shape),
                  _const_spec((1, 1024)), _const_spec((1, 256)), _const_spec((1, 256)),
                  _const_spec((1, 128))],
        out_specs=[row(1024), row(512), row(512), row(512), row(1024), row(128)],
        out_shape=[jax.ShapeDtypeStruct((m, 1024), BF16), jax.ShapeDtypeStruct((m, 512), F32),
                   jax.ShapeDtypeStruct((m, 512), F32), jax.ShapeDtypeStruct((m, 512), F32),
                   jax.ShapeDtypeStruct((m, 1024), F32), jax.ShapeDtypeStruct((m, 128), F32)],
        compiler_params=_cp("parallel"),
        name="nsa_proj",
    )(x, g, w, bd, qg, ksg, kwg, gb)


def _gdn_proj_body(x_ref, g_ref, w_ref, qkv_ref, sz_ref, ba_ref):
    h = _rms_rows(x_ref[...], g_ref[...])
    p = _dot(h.astype(BF16), w_ref[...])
    qkv_ref[...] = p[:, :3072]
    sz_ref[...] = _silu(p[:, 3072:4096])
    ba_ref[...] = p[:, 4096:4224]


def _gdn_proj(x, g, w):
    m = x.shape[0]
    tm = min(256, m)
    row = lambda n: pl.BlockSpec((tm, n), lambda i: (i, 0))
    return pl.pallas_call(
        _gdn_proj_body,
        grid=(m // tm,),
        in_specs=[row(1024), _const_spec((1, 1024)), _const_spec(w.shape)],
        out_specs=[row(3072), row(1024), row(128)],
        out_shape=[jax.ShapeDtypeStruct((m, 3072), F32), jax.ShapeDtypeStruct((m, 1024), F32),
                   jax.ShapeDtypeStruct((m, 128), F32)],
        compiler_params=_cp("parallel"),
        name="gdn_proj",
    )(x, g, w)


def _conf_proj_body(x_ref, g_ref, w_ref, u_ref, sz_ref):
    h = _rms_rows(x_ref[...], g_ref[...])
    p = _dot(h.astype(BF16), w_ref[...])
    u_ref[...] = p[:, :1024] * _sigmoid(p[:, 1024:2048])
    sz_ref[...] = _silu(p[:, 2048:3072])


def _conf_proj(x, g, w):
    m = x.shape[0]
    tm = min(256, m)
    row = lambda n: pl.BlockSpec((tm, n), lambda i: (i, 0))
    return pl.pallas_call(
        _conf_proj_body,
        grid=(m // tm,),
        in_specs=[row(1024), _const_spec((1, 1024)), _const_spec(w.shape)],
        out_specs=[row(1024), row(1024)],
        out_shape=[jax.ShapeDtypeStruct((m, 1024), F32), jax.ShapeDtypeStruct((m, 1024), F32)],
        compiler_params=_cp("parallel"),
        name="conf_proj",
    )(x, g, w)


def _out_body(x_ref, a_ref, sz_ref, w_ref, y_ref):
    a = (a_ref[...] * sz_ref[...]).astype(BF16)
    y_ref[...] = x_ref[...] + _dot(a, w_ref[...])


def _out_proj(x, a, sz, w):
    m = x.shape[0]
    tm = min(256, m)
    row = pl.BlockSpec((tm, 1024), lambda i: (i, 0))
    return pl.pallas_call(
        _out_body,
        grid=(m // tm,),
        in_specs=[row, row, row, _const_spec(w.shape)],
        out_specs=row,
        out_shape=jax.ShapeDtypeStruct((m, 1024), F32),
        compiler_params=_cp("parallel"),
        name="out_proj",
    )(x, a, sz, w)


def _nsa_out_body(x_ref, oc_ref, os_ref, ow_ref, gt_ref, sz_ref, ex_ref, w_ref, y_ref):
    gt = gt_ref[...]
    hi = gt.astype(BF16)
    lo = (gt - hi.astype(F32)).astype(BF16)
    ge = _dot(hi, ex_ref[...]) + _dot(lo, ex_ref[...])
    o = ge[:, :1024] * oc_ref[...] + ge[:, 1024:2048] * os_ref[...] + ge[:, 2048:] * ow_ref[...]
    a = (o * sz_ref[...]).astype(BF16)
    y_ref[...] = x_ref[...] + _dot(a, w_ref[...])


def _nsa_out(x, oc, os_, ow, gt, sz, ex, w):
    m = x.shape[0]
    tm = min(256, m)
    row = lambda n: pl.BlockSpec((tm, n), lambda i: (i, 0))
    return pl.pallas_call(
        _nsa_out_body,
        grid=(m // tm,),
        in_specs=[row(1024), row(1024), row(1024), row(1024), row(128), row(1024),
                  _const_spec(ex.shape), _const_spec(w.shape)],
        out_specs=row(1024),
        out_shape=jax.ShapeDtypeStruct((m, 1024), F32),
        compiler_params=_cp("parallel"),
        name="nsa_out",
    )(x, oc, os_, ow, gt, sz, ex, w)


CMP_PAGES = 32
CMP_ROWS = CMP_PAGES * PAGE_SIZE


def _cmp_proj_body(tbl_ref, *refs, transposed):
    page_refs = refs[:CMP_PAGES]
    w_ref, pe_ref, proj_ref, bias_ref, xs = refs[CMP_PAGES:]
    for k in range(CMP_PAGES):
        for c in range(4):
            if transposed:
                blk = page_refs[k][0, 0, c * LANE:(c + 1) * LANE, :].T
            else:
                blk = page_refs[k][0, :, c * LANE:(c + 1) * LANE]
            xs[c, k * PAGE_SIZE:(k + 1) * PAGE_SIZE, :] = blk
    for c in range(4):
        xs[c, CMP_ROWS:CMP_ROWS + PAGE_SIZE, :] = pe_ref[:, c * LANE:(c + 1) * LANE]
    n_out = (CMP_ROWS + PAGE_SIZE) // CMP_STRIDE
    acc = jnp.zeros((n_out, 1024), F32)
    for l in range(CMP_STRIDE):
        x_l = jnp.concatenate([xs[c, pl.ds(l, n_out, stride=CMP_STRIDE), :] for c in range(4)], axis=-1)
        acc = acc + _dot(x_l.astype(BF16), w_ref[l * 512:(l + 1) * 512, :])
    proj_ref[0] = acc[:CMP_ROWS // CMP_STRIDE]
    bias_ref[...] = acc[CMP_ROWS // CMP_STRIDE:]


def _cmp_proj(pool, table, wbig, pe, *, layer=None):
    nseq, npg = table.shape
    steps = npg // CMP_PAGES
    transposed = layer is not None

    def page_spec(k):
        if transposed:
            return pl.BlockSpec((1, 1, 512, PAGE_SIZE),
                                lambda s, j, tbl: (layer, tbl[s, j * CMP_PAGES + k], 0, 0))
        return pl.BlockSpec((1, PAGE_SIZE, 512), lambda s, j, tbl: (tbl[s, j * CMP_PAGES + k], 0, 0))

    n_out = CMP_ROWS // CMP_STRIDE
    grid_spec = pltpu.PrefetchScalarGridSpec(
        num_scalar_prefetch=1,
        grid=(nseq, steps),
        in_specs=[page_spec(k) for k in range(CMP_PAGES)] + [
            pl.BlockSpec((8192, 1024), lambda s, j, tbl: (0, 0), pipeline_mode=pl.Buffered(1)),
            pl.BlockSpec((PAGE_SIZE, 512), lambda s, j, tbl: (0, 0))],
        out_specs=[pl.BlockSpec((1, n_out, 1024), lambda s, j, tbl: (s, j, 0)),
                   pl.BlockSpec((8, 1024), lambda s, j, tbl: (0, 0))],
        scratch_shapes=[pltpu.VMEM((4, CMP_ROWS + PAGE_SIZE, LANE), F32)],
    )
    return pl.pallas_call(
        functools.partial(_cmp_proj_body, transposed=transposed),
        grid_spec=grid_spec,
        out_shape=[jax.ShapeDtypeStruct((nseq, N_CMP, 1024), F32),
                   jax.ShapeDtypeStruct((8, 1024), F32)],
        compiler_params=_cp("arbitrary", "arbitrary"),
        name="cmp_proj",
    )(table, *([pool] * CMP_PAGES), wbig, pe)


def _cmp_fin_body(proj_ref, bias_ref, bd_ref, kg_ref, out_ref, tmp):
    acc = proj_ref[0]
    nxt = pltpu.roll(acc[:, 512:], N_CMP - 1, axis=0)
    bias = bias_ref[0:1, 0:512] + bias_ref[1:2, 512:1024]
    comp = acc[:, :512] + nxt + bias
    kn = _seg_norm(comp[:, :256], bd_ref[...], kg_ref[...])
    for g in range(A_KV_GROUPS):
        tmp[g] = jnp.concatenate(
            [kn[:, g * 64:(g + 1) * 64], comp[:, 256 + g * 64:256 + (g + 1) * 64]], axis=-1)
        for j in range(4):
            out_ref[0, g, j * N_BLK:(j + 1) * N_BLK, :] = tmp[g, pl.ds(j, N_BLK, stride=4), :]


def _cmp_fin(proj, bias, bd4, kg):
    nseq = proj.shape[0]
    return pl.pallas_call(
        _cmp_fin_body,
        grid=(nseq,),
        in_specs=[pl.BlockSpec((1, N_CMP, 1024), lambda s: (s, 0, 0)), _const_spec((8, 1024)),
                  _const_spec((256, 256)), _const_spec((1, 256))],
        out_specs=pl.BlockSpec((1, A_KV_GROUPS, N_CMP, 128), lambda s: (s, 0, 0, 0)),
        out_shape=jax.ShapeDtypeStruct((nseq, A_KV_GROUPS, N_CMP, 128), F32),
        scratch_shapes=[pltpu.VMEM((A_KV_GROUPS, N_CMP, 128), F32)],
        compiler_params=_cp("parallel"),
        name="cmp_fin",
    )(proj, bias, bd4, kg)


def _slopes(g, shape, axis):
    r = lax.broadcasted_iota(jnp.int32, shape, axis)
    return jnp.exp2(-0.5 * (g * A_GROUP_SIZE + r + 1).astype(F32))


def _cmp_attn_body(q_ref, kcv_ref, oc_ref, sel_ref, *rest, qb, t0, n_pick, emit_bias):
    flag_ref = rest[0] if emit_bias else None
    g = pl.program_id(1)
    i = pl.program_id(2)
    R = A_GROUP_SIZE
    q = q_ref[0, 0].reshape(R * qb, 128)
    t = t0 + i * qb + lax.broadcasted_iota(jnp.int32, (1, qb, 1), 1)
    blk = lax.broadcasted_iota(jnp.int32, (1, 1, N_BLK), 2)
    slope = _slopes(g, (R, 1, 1), 0)
    logits, valids, kvs = [], [], []
    mx = jnp.full((R, qb, 1), NEG_INF, F32)
    for j in range(4):
        kv = kcv_ref[0, 0, j * N_BLK:(j + 1) * N_BLK, :].astype(BF16)
        cpos = SLC_BLOCK * blk + (CMP_STRIDE * j + CMP_LEN - 1)
        valid = cpos <= t
        lg = _dot_nt(q, kv).reshape(R, qb, N_BLK) - slope * (t - cpos).astype(F32)
        lg = jnp.where(valid, lg, NEG_INF)
        mx = jnp.maximum(mx, jnp.max(lg, axis=-1, keepdims=True))
        logits.append(lg)
        valids.append(valid)
        kvs.append(kv)
    ps = [jnp.where(v, jnp.exp(lg - mx), 0.0) for lg, v in zip(logits, valids)]
    den = sum(jnp.sum(p, axis=-1, keepdims=True) for p in ps) + TINY
    acc = jnp.zeros((R * qb, 128), F32)
    imp = jnp.zeros((qb, N_BLK), F32)
    for j in range(4):
        pc = ps[j] / den
        acc = acc + _dot(pc.reshape(R * qb, N_BLK).astype(BF16), kvs[j])
        imp = imp + jnp.sum(pc, axis=0)
    acc = acc.reshape(R, qb, 128)
    oc_ref[0] = jnp.concatenate([acc[r, :, 64:] for r in range(R)], axis=-1)

    square = qb == N_BLK
    axis = 0 if square else 1
    if square:
        imp = imp.T
        tq = t0 + i * qb + lax.broadcasted_iota(jnp.int32, (1, qb), 1)
        blk2 = lax.broadcasted_iota(jnp.int32, (N_BLK, 1), 0)
    else:
        tq = t[0]
        blk2 = blk[0]
    blkf = blk2.astype(F32)
    jt = tq // SLC_BLOCK
    forced = (blk2 == 0) | (blk2 == jt) | (blk2 == jt - 1)
    score = jnp.where(blk2 > jt, NEG_INF, imp + jnp.where(forced, FORCE_BONUS, 0.0))
    sel = jnp.zeros(score.shape, F32)
    for _ in range(n_pick):
        smax = jnp.max(score, axis=axis, keepdims=True)
        first = jnp.min(jnp.where(score == smax, blkf, float(N_BLK)), axis=axis, keepdims=True)
        pick = blkf == first
        sel = jnp.where(pick, 1.0, sel)
        score = jnp.where(pick, -jnp.inf, score)
    if square:
        sel = sel.T
    if emit_bias:
        sel_ref[0, 0] = ((sel - 1.0) * MASK_BIG).astype(BF16)
        any_blk = jnp.broadcast_to(jnp.max(sel, axis=0, keepdims=True), (SUBLANE, N_BLK))
        grp = (lax.broadcasted_iota(jnp.int32, (N_BLK, N_BLK), 0) // (SLC_TK // SLC_BLOCK)
               == lax.broadcasted_iota(jnp.int32, (N_BLK, N_BLK), 1))
        flag_ref[0, 0, 0] = _dot(any_blk.astype(BF16), jnp.where(grp, 1.0, 0.0).astype(BF16))
    else:
        sel_ref[0, 0] = sel


def _cmp_attn(qe, kcv, *, qb, t0, n_pick, emit_bias):
    ns, G, R, T, _ = qe.shape
    nqb = T // qb
    out_specs = [pl.BlockSpec((1, qb, 256), lambda b, g, i: (b, i, g)),
                 pl.BlockSpec((1, 1, qb, N_BLK), lambda b, g, i: (b, g, i, 0))]
    out_shape = [jax.ShapeDtypeStruct((ns, T, 1024), F32),
                 jax.ShapeDtypeStruct((ns, G, T, N_BLK), BF16 if emit_bias else F32)]
    if emit_bias:
        out_specs.append(pl.BlockSpec((1, 1, 1, SUBLANE, N_BLK), lambda b, g, i: (b, g, i, 0, 0)))
        out_shape.append(jax.ShapeDtypeStruct((ns, G, nqb, SUBLANE, N_BLK), F32))
    return pl.pallas_call(
        functools.partial(_cmp_attn_body, qb=qb, t0=t0, n_pick=n_pick, emit_bias=emit_bias),
        grid=(ns, G, nqb),
        in_specs=[pl.BlockSpec((1, 1, R, qb, 128), lambda b, g, i: (b, g, 0, i, 0)),
                  pl.BlockSpec((1, 1, N_CMP, 128), lambda b, g, i: (b, g, 0, 0))],
        out_specs=out_specs,
        out_shape=out_shape,
        compiler_params=_cp("parallel", "parallel", "parallel"),
        name="cmp_attn",
    )(qe, kcv)


SLC_TK = 256
WIN_TK = 128
QB = 128


N_KTILE = N_BLK * SLC_BLOCK // SLC_TK


def _p_attn_body(flag_ref, q_ref, kxs_ref, vxs_ref, kxw_ref, vxw_ref, os_ref, ow_ref, m_sc, acc_sc):
    b = pl.program_id(0)
    g = pl.program_id(1)
    i = pl.program_id(2)
    R = A_GROUP_SIZE
    rows = R * QB
    q = q_ref[0, 0].reshape(rows, 256)
    t = i * QB + lax.broadcasted_iota(jnp.int32, (rows, 1), 0) % QB

    def init():
        m_sc[...] = jnp.full((rows, 1), NEG_INF, F32)
        acc_sc[...] = jnp.zeros((rows, 128), F32)

    def step(qq, kx, vx, k0, keep):
        s = _dot_nt(qq, kx)
        if keep is not None:
            kpos = k0 + lax.broadcasted_iota(jnp.int32, (1, kx.shape[0]), 1)
            s = jnp.where(keep(t - kpos), s, -MASK_BIG)
        m_old = m_sc[...]
        m_new = jnp.maximum(m_old, jnp.max(s, axis=-1, keepdims=True))
        p = jnp.exp(s - m_new)
        acc_sc[...] = jnp.exp(m_old - m_new) * acc_sc[...] + _dot(p.astype(BF16), vx)
        m_sc[...] = m_new

    def finish(o_ref):
        acc = acc_sc[...]
        o = (acc[:, :64] / (acc[:, 64:65] + TINY)).reshape(R, QB, 64)
        o_ref[0] = jnp.concatenate([o[r] for r in range(R)], axis=-1)

    init()
    fbase = ((b * pl.num_programs(1) + g) * pl.num_programs(2) + i) * N_KTILE
    diag = i * QB // SLC_TK

    def slc_tile(kt, carry):
        @pl.when(flag_ref[fbase + kt] > 0)
        def _():
            k0 = pl.multiple_of(kt * SLC_TK, SLC_TK)
            step(q, kxs_ref[0, 0, pl.ds(k0, SLC_TK), :], vxs_ref[0, 0, pl.ds(k0, SLC_TK), :], k0, None)
        return carry

    lax.fori_loop(0, diag, slc_tile, 0)
    k0 = pl.multiple_of(diag * SLC_TK, SLC_TK)
    step(q, kxs_ref[0, 0, pl.ds(k0, SLC_TK), :], vxs_ref[0, 0, pl.ds(k0, SLC_TK), :], k0,
         lambda d: d >= 0)
    finish(os_ref)

    init()
    qw = q[:, :128]
    n_w = WINDOW // WIN_TK
    for w in range(n_w + 1):
        kt = i - n_w + w
        keep = (lambda d: d < WINDOW) if w == 0 else (lambda d: d >= 0) if w == n_w else None

        def win_tile(kt=kt, keep=keep):
            k0 = pl.multiple_of(jnp.maximum(kt, 0) * WIN_TK, WIN_TK)
            step(qw, kxw_ref[0, 0, pl.ds(k0, WIN_TK), :], vxw_ref[0, 0, pl.ds(k0, WIN_TK), :], k0, keep)

        if w == n_w:
            win_tile()
        else:
            pl.when(kt >= 0)(win_tile)
    finish(ow_ref)


def _p_attn(flags, qx, kxs, vxs, kxw, vxw):
    B, G, R, T, _ = qx.shape
    kv_spec = lambda n: pl.BlockSpec((1, 1, T, n), lambda b, g, i, fl: (b, g, 0, 0))
    o_spec = pl.BlockSpec((1, QB, 256), lambda b, g, i, fl: (b, i, g))
    grid_spec = pltpu.PrefetchScalarGridSpec(
        num_scalar_prefetch=1,
        grid=(B, G, T // QB),
        in_specs=[pl.BlockSpec((1, 1, R, QB, 256), lambda b, g, i, fl: (b, g, 0, i, 0)),
                  kv_spec(256), kv_spec(128), kv_spec(128), kv_spec(128)],
        out_specs=[o_spec, o_spec],
        scratch_shapes=[pltpu.VMEM((R * QB, 1), F32), pltpu.VMEM((R * QB, 128), F32)],
    )
    return pl.pallas_call(
        _p_attn_body,
        grid_spec=grid_spec,
        out_shape=[jax.ShapeDtypeStruct((B, T, 1024), F32)] * 2,
        compiler_params=_cp("parallel", "parallel", "arbitrary"),
        name="p_attn",
    )(flags, qx, kxs, vxs, kxw, vxw)


S_PAGES = 8
SQ = 8
S_ROWS = A_HEADS * SQ


def _s_rows_meta(past_len):
    row = lax.broadcasted_iota(jnp.int32, (S_ROWS, 1), 0)
    slope = jnp.exp2(-0.5 * (row // SQ + 1).astype(F32))
    t = past_len + row % SQ
    return slope, t


def _s_step(q, kvt, valid, bias, m, l, acc):
    kvt = kvt.astype(BF16)
    s = _dot(q, kvt[:256]) - bias
    s = jnp.where(valid, s, NEG_INF)
    mn = jnp.maximum(m, jnp.max(s, axis=-1, keepdims=True))
    alpha = jnp.exp(m - mn)
    p = jnp.where(valid, jnp.exp(s - mn), 0.0)
    l = alpha * l + jnp.sum(p, axis=-1, keepdims=True)
    return mn, l, alpha * acc + _dot_nt(p.astype(BF16), kvt[256:])


def _s_out(l, acc):
    o = acc / (l + TINY)
    pieces = []
    for g in range(A_KV_GROUPS):
        for r in range(A_GROUP_SIZE):
            r0 = (g * A_GROUP_SIZE + r) * SQ
            pieces.append(o[r0:r0 + SQ, g * 64:(g + 1) * 64])
    return jnp.concatenate(pieces, axis=-1)


def _s_attn_body(tbl_ref, *refs, past_len):
    page_refs = refs[:S_PAGES]
    q_ref, sel_ref, new_s_ref, win_ref, new_w_ref, os_ref, ow_ref, m_sc, l_sc, acc_sc = refs[S_PAGES:]
    j = pl.program_id(1)
    nj = pl.num_programs(1)
    q = q_ref[0]
    slope, t = _s_rows_meta(past_len)
    sel = sel_ref[0]
    selr = jnp.broadcast_to(sel[:, None], (A_KV_GROUPS, A_GROUP_SIZE, SQ, N_BLK))
    selr = selr.reshape(S_ROWS, N_BLK).astype(BF16)

    @pl.when(j == 0)
    def _():
        m_sc[...] = jnp.full((S_ROWS, 1), NEG_INF, F32)
        l_sc[...] = jnp.zeros((S_ROWS, 1), F32)
        acc_sc[...] = jnp.zeros((S_ROWS, 256), F32)

    m, l, acc = m_sc[...], l_sc[...], acc_sc[...]
    for k in range(S_PAGES):
        kv = page_refs[k][0, 0]
        k0 = (j * S_PAGES + k) * PAGE_SIZE
        kpos2 = k0 + lax.broadcasted_iota(jnp.int32, (N_BLK, PAGE_SIZE), 1)
        blk2 = lax.broadcasted_iota(jnp.int32, (N_BLK, PAGE_SIZE), 0)
        expand = jnp.where(kpos2 // SLC_BLOCK == blk2, 1.0, 0.0).astype(BF16)
        chosen = _dot(selr, expand)
        kpos = k0 + lax.broadcasted_iota(jnp.int32, (1, PAGE_SIZE), 1)
        valid = (chosen > 0.5) & (kpos <= t)
        m, l, acc = _s_step(q, kv, valid, slope * (t - kpos).astype(F32), m, l, acc)
    m_sc[...], l_sc[...], acc_sc[...] = m, l, acc

    @pl.when(j == nj - 1)
    def _():
        knew = past_len + lax.broadcasted_iota(jnp.int32, (1, PAGE_SIZE), 1)
        dnew = t - knew
        _, l2, acc2 = _s_step(q, new_s_ref[0], dnew >= 0, slope * dnew.astype(F32), m, l, acc)
        os_ref[0] = _s_out(l2, acc2)
        nbuf = win_ref.shape[3]
        wpos = (past_len - nbuf) + lax.broadcasted_iota(jnp.int32, (1, nbuf), 1)
        dw = t - wpos
        mw = jnp.full((S_ROWS, 1), NEG_INF, F32)
        lw = jnp.zeros((S_ROWS, 1), F32)
        aw = jnp.zeros((S_ROWS, 256), F32)
        mw, lw, aw = _s_step(q, win_ref[0, 0], (dw >= 0) & (dw < WINDOW),
                             slope * dw.astype(F32), mw, lw, aw)
        _, lw, aw = _s_step(q, new_w_ref[0], (dnew >= 0) & (dnew < WINDOW),
                            slope * dnew.astype(F32), mw, lw, aw)
        ow_ref[0] = _s_out(lw, aw)


def _s_attn(table, pool, qx, sel, new_s, win, new_w, *, layer, past_len):
    ns, npg = table.shape
    steps = npg // S_PAGES

    def page_spec(k):
        return pl.BlockSpec((1, 1, 512, PAGE_SIZE),
                            lambda s, j, tbl: (layer, tbl[s, j * S_PAGES + k], 0, 0))

    per_seq = lambda shape: pl.BlockSpec((1,) + shape, lambda s, j, tbl: (s,) + (0,) * len(shape))
    grid_spec = pltpu.PrefetchScalarGridSpec(
        num_scalar_prefetch=1,
        grid=(ns, steps),
        in_specs=[page_spec(k) for k in range(S_PAGES)] + [
            per_seq((S_ROWS, 256)), per_seq((A_KV_GROUPS, SQ, N_BLK)), per_seq((512, PAGE_SIZE)),
            pl.BlockSpec((1, 1, 512, win.shape[3]), lambda s, j, tbl: (layer, s, 0, 0)),
            per_seq((512, PAGE_SIZE))],
        out_specs=[per_seq((SQ, 1024)), per_seq((SQ, 1024))],
        scratch_shapes=[pltpu.VMEM((S_ROWS, 1), F32), pltpu.VMEM((S_ROWS, 1), F32),
                        pltpu.VMEM((S_ROWS, 256), F32)],
    )
    return pl.pallas_call(
        functools.partial(_s_attn_body, past_len=past_len),
        grid_spec=grid_spec,
        out_shape=[jax.ShapeDtypeStruct((ns, SQ, 1024), F32)] * 2,
        compiler_params=_cp("parallel", "arbitrary"),
        name="s_attn",
    )(table, *([pool] * S_PAGES), qx, sel, new_s, win, new_w)


def _gdn_body(qkv_ref, ba_ref, h0_ref, s0_ref, cw_ref, alog_ref, dt_ref, og_ref,
              o_ref, st_ref, buf, S, *, C, t_valid):
    i = pl.program_id(1)
    n_i = pl.num_programs(1)

    @pl.when(i == 0)
    def _():
        buf[0:8, :] = h0_ref[0]
        S[...] = s0_ref[0]

    buf[8:8 + C, :] = qkv_ref[0]
    y = jnp.zeros((C, 3 * DN_WIDTH), F32)
    for j in range(DN_CONV):
        y = y + cw_ref[j:j + 1, :] * buf[pl.ds(8 - (DN_CONV - 1) + j, C), :]
    y = _silu(y)
    if C >= 8:
        buf[0:8, :] = buf[C:C + 8, :]

    ba = ba_ref[0]
    sp_in = ba + dt_ref[...]
    softplus = jnp.maximum(sp_in, 0.0) + jnp.log(1.0 + jnp.exp(-jnp.abs(sp_in)))
    g_all = -jnp.exp(alog_ref[...]) * softplus
    beta_all = _sigmoid(ba)
    row = lax.broadcasted_iota(jnp.int32, (C, C), 0)
    col = lax.broadcasted_iota(jnp.int32, (C, C), 1)
    if t_valid < C:
        live = lax.broadcasted_iota(jnp.int32, (C, 1), 0) < t_valid
        g_all = jnp.where(live, g_all, 0.0)
        beta_all = jnp.where(live, beta_all, 0.0)
    tril = row >= col
    stril = row > col
    eye = jnp.where(row == col, 1.0, 0.0)
    ones = jnp.ones((C, C), F32)
    gc_all = jnp.dot(jnp.where(tril, 1.0, 0.0), g_all, precision=HI, preferred_element_type=F32)

    outs = []
    for h in range(DN_HEADS):
        qh = y[:, h * 128:(h + 1) * 128]
        kh = y[:, DN_WIDTH + h * 128:DN_WIDTH + (h + 1) * 128]
        vh = y[:, 2 * DN_WIDTH + h * 128:2 * DN_WIDTH + (h + 1) * 128]
        qh = qh * lax.rsqrt(jnp.sum(qh * qh, axis=-1, keepdims=True) + NORM_EPS) * (DN_HEAD_DIM ** -0.5)
        kh = kh * lax.rsqrt(jnp.sum(kh * kh, axis=-1, keepdims=True) + NORM_EPS)
        bh = beta_all[:, h:h + 1]
        gc = gc_all[:, 8 + h:9 + h]
        eg = jnp.exp(gc)
        gc_row = jnp.dot(ones, eye * gc, precision=HI, preferred_element_type=F32)
        decay = jnp.where(tril, jnp.exp(jnp.where(tril, gc - gc_row, 0.0)), 0.0)
        kb = kh * bh
        khb = kh.astype(BF16)
        a_mat = jnp.where(stril, _dot_nt(kb.astype(BF16), khb) * decay, 0.0)
        pw = -a_mat
        t_mat = eye + pw
        for _ in range(max(int(math.log2(C)) - 1, 0)):
            pw = _dot3(pw, pw)
            t_mat = t_mat + _dot3(t_mat, pw)
        t_b = t_mat.astype(BF16)
        u = _dot(t_b, (vh * bh).astype(BF16))
        w = _dot(t_b, (kb * eg).astype(BF16))
        qk = jnp.where(tril, _dot_nt(qh.astype(BF16), khb) * decay, 0.0)
        s_h = S[h]
        s_b = s_h.astype(BF16)
        v_new = u - _dot(w.astype(BF16), s_b)
        v_b = v_new.astype(BF16)
        o_h = _dot((qh * eg).astype(BF16), s_b) + _dot(qk.astype(BF16), v_b)
        g_last = gc[C - 1:C, :]
        k_dec = (kh * jnp.exp(g_last - gc)).T.astype(BF16)
        S[h] = s_h * jnp.exp(g_last) + _dot(k_dec, v_b)
        outs.append(_rms_rows(o_h, og_ref[...]))
    o_ref[0] = jnp.concatenate(outs, axis=-1)

    @pl.when(i == n_i - 1)
    def _():
        st_ref[0] = S[...]


def _gdn_core(qkv, ba, h0, s0, cw, alog, dt, og, *, C, t_valid):
    B, T, _ = qkv.shape
    per_b = lambda shape: pl.BlockSpec((1,) + shape, lambda b, i: (b,) + (0,) * len(shape))
    tile = lambda n: pl.BlockSpec((1, C, n), lambda b, i: (b, i, 0))
    return pl.pallas_call(
        functools.partial(_gdn_body, C=C, t_valid=t_valid),
        grid=(B, T // C),
        in_specs=[tile(3 * DN_WIDTH), tile(128), per_b((8, 3 * DN_WIDTH)),
                  per_b((DN_HEADS, 128, 128)), _const_spec((8, 3 * DN_WIDTH)),
                  _const_spec((1, 128)), _const_spec((1, 128)), _const_spec((1, 128))],
        out_specs=[tile(DN_WIDTH), per_b((DN_HEADS, 128, 128))],
        out_shape=[jax.ShapeDtypeStruct((B, T, DN_WIDTH), F32),
                   jax.ShapeDtypeStruct((B, DN_HEADS, 128, 128), F32)],
        scratch_shapes=[pltpu.VMEM((8 + C, 3 * DN_WIDTH), F32),
                        pltpu.VMEM((DN_HEADS, 128, 128), F32)],
        compiler_params=_cp("parallel", "arbitrary"),
        name="gdn_core",
    )(qkv, ba, h0, s0, cw, alog, dt, og)


CONF_HIST = 32


def _conf_body(u_ref, h0_ref, cw_ref, cb_ref, lg_ref, lb_ref, c_ref, buf, *, tm, carry):
    i = pl.program_id(1)

    @pl.when(i == 0)
    def _():
        buf[0:CONF_HIST, :] = h0_ref[0]

    buf[CONF_HIST:CONF_HIST + tm, :] = u_ref[0]
    acc = jnp.zeros((tm, 1024), F32)
    for j in range(CONF_KERNEL):
        acc = acc + cw_ref[j:j + 1, :] * buf[pl.ds(CONF_HIST - (CONF_KERNEL - 1) + j, tm), :]
    c = acc + cb_ref[...]
    mu = jnp.mean(c, axis=-1, keepdims=True)
    var = jnp.mean(jnp.square(c - mu), axis=-1, keepdims=True)
    c = (c - mu) * lax.rsqrt(var + NORM_EPS) * lg_ref[...] + lb_ref[...]
    c_ref[0] = _silu(c)
    if carry:
        buf[0:CONF_HIST, :] = buf[tm:tm + CONF_HIST, :]


def _conf_conv(u, h0, cw, cb, lg, lb):
    B, T, _ = u.shape
    tm = min(256, T)
    return pl.pallas_call(
        functools.partial(_conf_body, tm=tm, carry=T > tm),
        grid=(B, T // tm),
        in_specs=[pl.BlockSpec((1, tm, 1024), lambda b, i: (b, i, 0)),
                  pl.BlockSpec((1, CONF_HIST, 1024), lambda b, i: (b, 0, 0)),
                  _const_spec((32, 1024)), _const_spec((1, 1024)), _const_spec((1, 1024)),
                  _const_spec((1, 1024))],
        out_specs=pl.BlockSpec((1, tm, 1024), lambda b, i: (b, i, 0)),
        out_shape=jax.ShapeDtypeStruct((B, T, 1024), F32),
        scratch_shapes=[pltpu.VMEM((CONF_HIST + tm, 1024), F32)],
        compiler_params=_cp("parallel", "arbitrary"),
        name="conf_conv",
    )(u, h0, cw, cb, lg, lb)


def _block_diag_ones(n, seg):
    r = jnp.arange(n) // seg
    return (r[:, None] == r[None, :]).astype(BF16)


def _nsa_weights(w_in, q_gain, k_gain, cmp_pos, cmp_w, gate_b, w_out):
    q_w, kvc_w, kvs_w, kvw_w, gl_w, z_w = jnp.split(
        w_in, [1024, 1536, 2048, 2560, 2608], axis=1)
    wcat = jnp.concatenate([q_w, kvc_w, kvs_w, kvw_w, z_w, jnp.pad(gl_w, ((0, 0), (0, 80)))],
                           axis=1).astype(BF16)
    w5 = cmp_w.reshape(2, 2, CMP_STRIDE, A_HEAD_DIM, A_HEAD_DIM)
    eye_c = jnp.eye(2, dtype=F32)
    eye_g = jnp.eye(A_KV_GROUPS, dtype=F32)
    wbig = jnp.einsum('chlde,ca,gb->labdhcge', w5, eye_c, eye_g).reshape(8192, 1024).astype(BF16)
    pe4 = cmp_pos.reshape(2, 2, CMP_STRIDE, A_HEAD_DIM)
    pe = jnp.broadcast_to(pe4.transpose(1, 2, 0, 3)[:, :, :, None, :],
                          (2, CMP_STRIDE, 2, A_KV_GROUPS, A_HEAD_DIM)).reshape(2 * CMP_STRIDE, 512)
    pe = jnp.pad(pe, ((0, PAGE_SIZE - 2 * CMP_STRIDE), (0, 0)))
    heads = jnp.arange(1024) // A_HEAD_DIM
    ex = jnp.concatenate(
        [(jnp.arange(128)[:, None] == (br * A_HEADS + heads)[None, :]) for br in range(3)],
        axis=1).astype(BF16)
    return dict(
        wcat=wcat, wbig=wbig, pe=pe, ex=ex,
        bd=_block_diag_ones(1024, A_HEAD_DIM),
        qg=(jnp.tile(q_gain, A_HEADS) * (A_HEAD_DIM ** -0.5))[None, :],
        kcg=jnp.tile(k_gain[0], A_KV_GROUPS)[None, :],
        ksg=jnp.tile(k_gain[1], A_KV_GROUPS)[None, :],
        kwg=jnp.tile(k_gain[2], A_KV_GROUPS)[None, :],
        gb=jnp.pad(gate_b, (0, 80))[None, :],
        w_out=w_out.astype(BF16))


def _split3(x):
    p1 = x.astype(BF16)
    p2 = (x - p1.astype(F32)).astype(BF16)
    p3 = (x - p1.astype(F32) - p2.astype(F32)).astype(BF16)
    return p1, p2, p3


def _alibi_query_feats():
    slope = jnp.exp2(-8.0 * jnp.arange(1, A_HEADS + 1, dtype=F32) / A_HEADS)
    s1, s2, s3 = _split3(slope)
    sc = jnp.asarray(SLC_BLOCK, BF16)
    f = jnp.stack([s1 * sc, s2 * sc, s3 * sc, s1, s2, s3], axis=-1)
    return jnp.pad(f, ((0, 0), (0, 58))).reshape(A_KV_GROUPS, A_GROUP_SIZE, 64)


def _key_pos_feats(T):
    kp = jnp.arange(T, dtype=jnp.int32)
    a = (kp // SLC_BLOCK).astype(BF16)
    b = (kp % SLC_BLOCK).astype(BF16)
    return jnp.pad(jnp.stack([a, a, a, b, b, b], axis=-1), ((0, 0), (0, 58)))


def _prompt_kv_feats(kv, B, T, with_blocks):
    kv = kv.reshape(B, T, 2, A_KV_GROUPS, A_HEAD_DIM).transpose(2, 0, 3, 1, 4).astype(BF16)
    lead = (B, A_KV_GROUPS, T)
    parts = [kv[0], jnp.broadcast_to(_key_pos_feats(T), lead + (64,))]
    if with_blocks:
        onehot = (jnp.arange(T)[:, None] // SLC_BLOCK == jnp.arange(N_BLK)[None, :]).astype(BF16)
        parts.append(jnp.broadcast_to(onehot, lead + (N_BLK,)))
    ones = jnp.pad(jnp.ones(lead + (1,), BF16), ((0, 0),) * 3 + ((0, 63),))
    return jnp.concatenate(parts, axis=-1), jnp.concatenate([kv[1], ones], axis=-1)


def _feature_major(cache):
    L, n, rows = cache.shape[:3]
    return cache.transpose(0, 1, 3, 4, 5, 2).reshape(L, n, A_KV_WIDTH, rows)


def _nsa_layer(y, g_norm, wts, B, T, *, past=None):
    M = B * T
    x2 = y.reshape(M, D_MODEL)
    qn, kvc, kvs, kvw, sz, gt = _nsa_proj(x2, g_norm, wts['wcat'], wts['bd'], wts['qg'],
                                           wts['ksg'], wts['kwg'], wts['gb'])
    bd4 = wts['bd'][:256, :256]
    q5 = qn.reshape(B, T, A_KV_GROUPS, A_GROUP_SIZE, A_HEAD_DIM).transpose(0, 2, 3, 1, 4)
    if past is None:
        n_pg = T // PAGE_SIZE
        table = jnp.arange(B * n_pg, dtype=jnp.int32).reshape(B, n_pg)
        proj, bias = _cmp_proj(kvc.reshape(B * n_pg, PAGE_SIZE, A_KV_WIDTH), table,
                               wts['wbig'], wts['pe'])
        kcv = _cmp_fin(proj, bias, bd4, wts['kcg'])
        qe = jnp.pad(q5, ((0, 0),) * 4 + ((0, 64),))
        oc, selb, flags = _cmp_attn(qe, kcv, qb=QB, t0=0, n_pick=N_SELECT, emit_bias=True)
        flags = (flags[:, :, :, 0, :N_KTILE] > 0).astype(jnp.int32).reshape(-1)
        lead = (B, A_KV_GROUPS, A_GROUP_SIZE, T)
        qx = jnp.concatenate([q5, jnp.broadcast_to(_alibi_query_feats()[None, :, :, None, :], lead + (64,)),
                              jnp.broadcast_to(selb[:, :, None], lead + (N_BLK,))], axis=-1)
        kxs, vxs = _prompt_kv_feats(kvs, B, T, True)
        kxw, vxw = _prompt_kv_feats(kvw, B, T, False)
        os_, ow = _p_attn(flags, qx, kxs, vxs, kxw, vxw)
        oc, os_, ow = (o.reshape(M, 1024) for o in (oc, os_, ow))
        new_win = kvw.reshape(B, T, A_KV_WIDTH)[:, T - min(WINDOW, T):]
    else:
        cmp_t, slc_t, win_t, win_buf, layer, table, past_len = past
        proj, bias = _cmp_proj(cmp_t, table, wts['wbig'], wts['pe'], layer=layer)
        kcv = _cmp_fin(proj, bias, bd4, wts['kcg'])
        q5p = jnp.pad(q5, ((0, 0), (0, 0), (0, 0), (0, SQ - T), (0, 0)))
        qe = jnp.pad(q5p, ((0, 0),) * 4 + ((0, 64),))
        oc, sel = _cmp_attn(qe, kcv, qb=SQ, t0=past_len, n_pick=N_SELECT - 1, emit_bias=False)
        eye_g = jnp.eye(A_KV_GROUPS, dtype=BF16)
        qx = jnp.einsum('bgrqd,gh->bgrqhd', q5p, eye_g).reshape(B, S_ROWS, 256)
        new_t = lambda a: jnp.pad(a.reshape(B, T, A_KV_WIDTH).transpose(0, 2, 1),
                                  ((0, 0), (0, 0), (0, PAGE_SIZE - T)))
        os_, ow = _s_attn(table, slc_t, qx, sel, new_t(kvs), win_t, new_t(kvw),
                          layer=layer, past_len=past_len)
        oc, os_, ow = (o[:, :T].reshape(M, 1024) for o in (oc, os_, ow))
        w_rows = win_buf.reshape(B, -1, A_KV_WIDTH)
        full_w = jnp.concatenate([w_rows, kvw.reshape(B, T, A_KV_WIDTH)], axis=1)
        new_win = full_w[:, full_w.shape[1] - w_rows.shape[1]:]
    y_new = _nsa_out(x2, oc, os_, ow, gt, sz, wts['ex'], wts['w_out']).reshape(B, T, D_MODEL)
    shape6 = lambda a: a.reshape(B, -1, 2, A_KV_GROUPS, A_HEAD_DIM)
    return y_new, shape6(kvc), shape6(kvs), shape6(new_win)


def _gdn_layer(y, g_norm, w_in, conv_w, a_log, dt_bias, o_gain, w_out, B, T, conv_buf, s0):
    M = B * T
    x2 = y.reshape(M, D_MODEL)
    wcat = jnp.pad(w_in, ((0, 0), (0, 112))).astype(BF16)
    qkv, sz, ba = _gdn_proj(x2, g_norm, wcat)
    qkv3 = qkv.reshape(B, T, 3 * DN_WIDTH)
    C = DN_CHUNK if T >= DN_CHUNK else 8
    Tp = -(-T // C) * C
    pad_t = lambda a: jnp.pad(a, ((0, 0), (0, Tp - T), (0, 0)))
    h0 = jnp.pad(conv_buf, ((0, 0), (8 - (DN_CONV - 1), 0), (0, 0)))
    lane8 = lambda v: jnp.pad(v, (8, 112))[None, :]
    o, s_new = _gdn_core(pad_t(qkv3), pad_t(ba.reshape(B, T, 128)), h0, s0,
                         jnp.pad(conv_w, ((0, 8 - DN_CONV), (0, 0))), lane8(a_log), lane8(dt_bias),
                         o_gain[None, :], C=C, t_valid=min(T, C) if Tp != T else C)
    y_new = _out_proj(x2, o[:, :T].reshape(M, DN_WIDTH), sz, w_out.astype(BF16))
    x_ext_tail = jnp.concatenate([conv_buf, qkv3], axis=1)[:, T:]
    return y_new.reshape(B, T, D_MODEL), s_new, x_ext_tail


def _conf_layer(y, g_norm, w_in, conv_w, conv_b, ln_g, ln_b, w_out, B, T, conv_buf):
    M = B * T
    x2 = y.reshape(M, D_MODEL)
    u, sz = _conf_proj(x2, g_norm, w_in.astype(BF16))
    u3 = u.reshape(B, T, 1024)
    h0 = jnp.pad(conv_buf, ((0, 0), (CONF_HIST - (CONF_KERNEL - 1), 0), (0, 0)))
    c = _conf_conv(u3, h0, jnp.pad(conv_w, ((0, 32 - CONF_KERNEL), (0, 0))), conv_b[None, :],
                   ln_g[None, :], ln_b[None, :])
    y_new = _out_proj(x2, c.reshape(M, 1024), sz, w_out.astype(BF16))
    x_ext_tail = jnp.concatenate([conv_buf, u3], axis=1)[:, T:]
    return y_new.reshape(B, T, D_MODEL), x_ext_tail


def kernel(x_prompt, x_sample, cache_cmp_kv, cache_slc_kv, cache_win_kv, state_delta, state_delta_conv, state_conv, page_table, norm_g, a_w_in, a_q_gain, a_k_gain, a_cmp_pos, a_cmp_w, a_gate_b, a_w_out, b_w_in, b_conv_w, b_a_log, b_dt_bias, b_o_gain, b_w_out, c_w_in, c_conv_w, c_conv_b, c_ln_g, c_ln_b, c_w_out):
    depth = norm_g.shape[0]
    bp, tp, _ = x_prompt.shape
    bs, ts, _ = x_sample.shape
    past_len = page_table.shape[1] * PAGE_SIZE
    yp, ys = x_prompt, x_sample
    cmp_t, slc_t, win_t = (_feature_major(c) for c in (cache_cmp_kv, cache_slc_kv, cache_win_kv))
    outs = [[] for _ in range(12)]
    for i in range(depth):
        j, kind = divmod(i, N_MIXERS)
        gn = norm_g[i][None, :]
        if kind == 0:
            wts = _nsa_weights(a_w_in[j], a_q_gain[j], a_k_gain[j], a_cmp_pos[j], a_cmp_w[j],
                               a_gate_b[j], a_w_out[j])
            yp, cp_, sp_, wp_ = _nsa_layer(yp, gn, wts, bp, tp)
            ys, cs_, ss_, ws_ = _nsa_layer(
                ys, gn, wts, bs, ts,
                past=(cmp_t, slc_t, win_t, cache_win_kv[j], j, page_table, past_len))
            for lst, v in zip(outs[0:6], (cp_, cs_, sp_, ss_, wp_, ws_)):
                lst.append(v)
        elif kind == 1:
            wb = (b_w_in[j], b_conv_w[j], b_a_log[j], b_dt_bias[j], b_o_gain[j], b_w_out[j])
            yp, st_p, cv_p = _gdn_layer(yp, gn, *wb, bp, tp,
                                        jnp.zeros((bp, DN_CONV - 1, 3 * DN_WIDTH), F32),
                                        jnp.zeros((bp, DN_HEADS, DN_HEAD_DIM, DN_HEAD_DIM), F32))
            ys, st_s, cv_s = _gdn_layer(ys, gn, *wb, bs, ts, state_delta_conv[j], state_delta[j])
            for lst, v in zip(outs[6:10], (st_p, st_s, cv_p, cv_s)):
                lst.append(v)
        else:
            wc = (c_w_in[j], c_conv_w[j], c_conv_b[j], c_ln_g[j], c_ln_b[j], c_w_out[j])
            yp, cb_p = _conf_layer(yp, gn, *wc, bp, tp, jnp.zeros((bp, CONF_KERNEL - 1, 1024), F32))
            ys, cb_s = _conf_layer(ys, gn, *wc, bs, ts, state_conv[j])
            for lst, v in zip(outs[10:12], (cb_p, cb_s)):
                lst.append(v)
    return (yp, ys) + tuple(jnp.stack(lst) for lst in outs)
```

```python
import functools
import math

import jax
import jax.numpy as jnp
from jax import lax
from jax.experimental import pallas as pl
from jax.experimental.pallas import tpu as pltpu

F32 = jnp.float32
BF16 = jnp.bfloat16
HI = lax.Precision.HIGHEST

D_MODEL = 1024
N_MIXERS = 3
A_HEADS = 16
A_HEAD_DIM = 64
A_KV_GROUPS = 4
A_GROUP_SIZE = 4
A_KV_WIDTH = 2 * A_KV_GROUPS * A_HEAD_DIM
CMP_STRIDE = 16
CMP_LEN = 32
SLC_BLOCK = 64
N_SELECT = 16
WINDOW = 512
PAGE_SIZE = 128
N_CMP = 512
N_BLK = 128
DN_HEADS = 8
DN_HEAD_DIM = 128
DN_WIDTH = 1024
DN_CONV = 4
DN_CHUNK = 64
DN_GROUP = 4
CONF_KERNEL = 31

NORM_EPS = 1e-6
NEG_INF = -1e30
FORCE_BONUS = 1e4
TINY = 1e-30
MASK_BIG = 2.0 ** 40

LANE = 128
SUBLANE = 8
VMEM_LIMIT = 56 * 1024 * 1024


def _cp(*sem):
    return pltpu.CompilerParams(dimension_semantics=sem, vmem_limit_bytes=VMEM_LIMIT)


def _dot(a, b):
    return jnp.dot(a, b, preferred_element_type=F32)


def _dot_nt(a, b, precision=None):
    return lax.dot_general(a, b, (((1,), (1,)), ((), ())), precision=precision,
                           preferred_element_type=F32)


def _dot_tn(a, b, precision=None):
    return lax.dot_general(a, b, (((0,), (0,)), ((), ())), precision=precision,
                           preferred_element_type=F32)


def _sigmoid(x):
    return 1.0 / (1.0 + jnp.exp(-x))


def _silu(x):
    return x * _sigmoid(x)


def _rms_rows(x, g):
    return x * lax.rsqrt(jnp.mean(x * x, axis=-1, keepdims=True) + NORM_EPS) * g


def _seg_norm(v, bd, gain):
    ss = _dot((v * v).astype(BF16), bd)
    return v * lax.rsqrt(ss * (1.0 / A_HEAD_DIM) + NORM_EPS) * gain


def _const_spec(shape):
    nd = len(shape)
    return pl.BlockSpec(shape, lambda *_: (0,) * nd)


def _nsa_proj_body(x_ref, g_ref, w_ref, bd_ref, qg_ref, ksg_ref, kwg_ref, gb_ref,
                   q_ref, kvc_ref, kvs_ref, kvw_ref, sz_ref, gt_ref):
    h = _rms_rows(x_ref[...], g_ref[...])
    p = _dot(h.astype(BF16), w_ref[...])
    bd = bd_ref[...]
    bd4 = bd[:256, :256]
    q_ref[...] = _seg_norm(p[:, :1024], bd, qg_ref[...]).astype(BF16)
    kvc_ref[...] = p[:, 1024:1536]
    kvs_ref[:, :256] = _seg_norm(p[:, 1536:1792], bd4, ksg_ref[...])
    kvs_ref[:, 256:] = p[:, 1792:2048]
    kvw_ref[:, :256] = _seg_norm(p[:, 2048:2304], bd4, kwg_ref[...])
    kvw_ref[:, 256:] = p[:, 2304:2560]
    sz_ref[...] = _silu(p[:, 2560:3584])
    gt_ref[...] = _sigmoid(p[:, 3584:3712] + gb_ref[...])


def _nsa_proj(x, g, w, bd, qg, ksg, kwg, gb):
    m = x.shape[0]
    tm = min(256, m)
    row = lambda n: pl.BlockSpec((tm, n), lambda i: (i, 0))
    return pl.pallas_call(
        _nsa_proj_body,
        grid=(m // tm,),
        in_specs=[row(1024), _const_spec((1, 1024)), _const_spec(w.shape), _const_spec(bd.shape),
                  _const_spec((1, 1024)), _const_spec((1, 256)), _const_spec((1, 256)),
                  _const_spec((1, 128))],
        out_specs=[row(1024), row(512), row(512), row(512), row(1024), row(128)],
        out_shape=[jax.ShapeDtypeStruct((m, 1024), BF16), jax.ShapeDtypeStruct((m, 512), F32),
                   jax.ShapeDtypeStruct((m, 512), F32), jax.ShapeDtypeStruct((m, 512), F32),
                   jax.ShapeDtypeStruct((m, 1024), F32), jax.ShapeDtypeStruct((m, 128), F32)],
        compiler_params=_cp("parallel"),
        name="nsa_proj",
    )(x, g, w, bd, qg, ksg, kwg, gb)


def _gdn_proj_body(x_ref, g_ref, w_ref, qkv_ref, sz_ref, ba_ref):
    h = _rms_rows(x_ref[...], g_ref[...])
    p = _dot(h.astype(BF16), w_ref[...])
    qkv_ref[...] = p[:, :3072]
    sz_ref[...] = _silu(p[:, 3072:4096])
    ba_ref[...] = p[:, 4096:4224]


def _gdn_proj(x, g, w):
    m = x.shape[0]
    tm = min(256, m)
    row = lambda n: pl.BlockSpec((tm, n), lambda i: (i, 0))
    return pl.pallas_call(
        _gdn_proj_body,
        grid=(m // tm,),
        in_specs=[row(1024), _const_spec((1, 1024)), _const_spec(w.shape)],
        out_specs=[row(3072), row(1024), row(128)],
        out_shape=[jax.ShapeDtypeStruct((m, 3072), F32), jax.ShapeDtypeStruct((m, 1024), F32),
                   jax.ShapeDtypeStruct((m, 128), F32)],
        compiler_params=_cp("parallel"),
        name="gdn_proj",
    )(x, g, w)


def _conf_proj_body(x_ref, g_ref, w_ref, u_ref, sz_ref):
    h = _rms_rows(x_ref[...], g_ref[...])
    p = _dot(h.astype(BF16), w_ref[...])
    u_ref[...] = p[:, :1024] * _sigmoid(p[:, 1024:2048])
    sz_ref[...] = _silu(p[:, 2048:3072])


def _conf_proj(x, g, w):
    m = x.shape[0]
    tm = min(256, m)
    row = lambda n: pl.BlockSpec((tm, n), lambda i: (i, 0))
    return pl.pallas_call(
        _conf_proj_body,
        grid=(m // tm,),
        in_specs=[row(1024), _const_spec((1, 1024)), _const_spec(w.shape)],
        out_specs=[row(1024), row(1024)],
        out_shape=[jax.ShapeDtypeStruct((m, 1024), F32), jax.ShapeDtypeStruct((m, 1024), F32)],
        compiler_params=_cp("parallel"),
        name="conf_proj",
    )(x, g, w)


def _out_body(x_ref, a_ref, sz_ref, w_ref, y_ref):
    a = (a_ref[...] * sz_ref[...]).astype(BF16)
    y_ref[...] = x_ref[...] + _dot(a, w_ref[...])


def _out_proj(x, a, sz, w):
    m = x.shape[0]
    tm = min(256, m)
    row = pl.BlockSpec((tm, 1024), lambda i: (i, 0))
    return pl.pallas_call(
        _out_body,
        grid=(m // tm,),
        in_specs=[row, row, row, _const_spec(w.shape)],
        out_specs=row,
        out_shape=jax.ShapeDtypeStruct((m, 1024), F32),
        compiler_params=_cp("parallel"),
        name="out_proj",
    )(x, a, sz, w)


def _nsa_out_body(x_ref, oc_ref, os_ref, ow_ref, gt_ref, sz_ref, ex_ref, w_ref, y_ref):
    gt = gt_ref[...]
    hi = gt.astype(BF16)
    lo = (gt - hi.astype(F32)).astype(BF16)
    ge = _dot(hi, ex_ref[...]) + _dot(lo, ex_ref[...])
    o = ge[:, :1024] * oc_ref[...] + ge[:, 1024:2048] * os_ref[...] + ge[:, 2048:] * ow_ref[...]
    a = (o * sz_ref[...]).astype(BF16)
    y_ref[...] = x_ref[...] + _dot(a, w_ref[...])


def _nsa_out(x, oc, os_, ow, gt, sz, ex, w):
    m = x.shape[0]
    tm = min(256, m)
    row = lambda n: pl.BlockSpec((tm, n), lambda i: (i, 0))
    return pl.pallas_call(
        _nsa_out_body,
        grid=(m // tm,),
        in_specs=[row(1024), row(1024), row(1024), row(1024), row(128), row(1024),
                  _const_spec(ex.shape), _const_spec(w.shape)],
        out_specs=row(1024),
        out_shape=jax.ShapeDtypeStruct((m, 1024), F32),
        compiler_params=_cp("parallel"),
        name="nsa_out",
    )(x, oc, os_, ow, gt, sz, ex, w)


CMP_PAGES = 32
CMP_ROWS = CMP_PAGES * PAGE_SIZE


def _cmp_proj_body(tbl_ref, *refs, transposed):
    page_refs = refs[:CMP_PAGES]
    w_ref, pe_ref, proj_ref, bias_ref, xs = refs[CMP_PAGES:]
    for k in range(CMP_PAGES):
        for c in range(4):
            if transposed:
                blk = page_refs[k][0, 0, c * LANE:(c + 1) * LANE, :].T
            else:
                blk = page_refs[k][0, :, c * LANE:(c + 1) * LANE]
            xs[c, k * PAGE_SIZE:(k + 1) * PAGE_SIZE, :] = blk
    for c in range(4):
        xs[c, CMP_ROWS:CMP_ROWS + PAGE_SIZE, :] = pe_ref[:, c * LANE:(c + 1) * LANE]
    n_out = (CMP_ROWS + PAGE_SIZE) // CMP_STRIDE
    acc = jnp.zeros((n_out, 1024), F32)
    for l in range(CMP_STRIDE):
        x_l = jnp.concatenate([xs[c, pl.ds(l, n_out, stride=CMP_STRIDE), :] for c in range(4)], axis=-1)
        acc = acc + _dot(x_l.astype(BF16), w_ref[l * 512:(l + 1) * 512, :])
    proj_ref[0] = acc[:CMP_ROWS // CMP_STRIDE]
    bias_ref[...] = acc[CMP_ROWS // CMP_STRIDE:]


def _cmp_proj(pool, table, wbig, pe, *, layer=None):
    nseq, npg = table.shape
    steps = npg // CMP_PAGES
    transposed = layer is not None

    def page_spec(k):
        if transposed:
            return pl.BlockSpec((1, 1, 512, PAGE_SIZE),
                                lambda s, j, tbl: (layer, tbl[s, j * CMP_PAGES + k], 0, 0))
        return pl.BlockSpec((1, PAGE_SIZE, 512), lambda s, j, tbl: (tbl[s, j * CMP_PAGES + k], 0, 0))

    n_out = CMP_ROWS // CMP_STRIDE
    grid_spec = pltpu.PrefetchScalarGridSpec(
        num_scalar_prefetch=1,
        grid=(nseq, steps),
        in_specs=[page_spec(k) for k in range(CMP_PAGES)] + [
            pl.BlockSpec((8192, 1024), lambda s, j, tbl: (0, 0), pipeline_mode=pl.Buffered(1)),
            pl.BlockSpec((PAGE_SIZE, 512), lambda s, j, tbl: (0, 0))],
        out_specs=[pl.BlockSpec((1, n_out, 1024), lambda s, j, tbl: (s, j, 0)),
                   pl.BlockSpec((8, 1024), lambda s, j, tbl: (0, 0))],
        scratch_shapes=[pltpu.VMEM((4, CMP_ROWS + PAGE_SIZE, LANE), F32)],
    )
    return pl.pallas_call(
        functools.partial(_cmp_proj_body, transposed=transposed),
        grid_spec=grid_spec,
        out_shape=[jax.ShapeDtypeStruct((nseq, N_CMP, 1024), F32),
                   jax.ShapeDtypeStruct((8, 1024), F32)],
        compiler_params=_cp("arbitrary", "arbitrary"),
        name="cmp_proj",
    )(table, *([pool] * CMP_PAGES), wbig, pe)


def _cmp_fin_body(proj_ref, bias_ref, bd_ref, kg_ref, out_ref, tmp):
    acc = proj_ref[0]
    nxt = pltpu.roll(acc[:, 512:], N_CMP - 1, axis=0)
    bias = bias_ref[0:1, 0:512] + bias_ref[1:2, 512:1024]
    comp = acc[:, :512] + nxt + bias
    kn = _seg_norm(comp[:, :256], bd_ref[...], kg_ref[...])
    for g in range(A_KV_GROUPS):
        tmp[g] = jnp.concatenate(
            [kn[:, g * 64:(g + 1) * 64], comp[:, 256 + g * 64:256 + (g + 1) * 64]], axis=-1)
        for j in range(4):
            out_ref[0, g, j * N_BLK:(j + 1) * N_BLK, :] = tmp[g, pl.ds(j, N_BLK, stride=4), :]


def _cmp_fin(proj, bias, bd4, kg):
    nseq = proj.shape[0]
    return pl.pallas_call(
        _cmp_fin_body,
        grid=(nseq,),
        in_specs=[pl.BlockSpec((1, N_CMP, 1024), lambda s: (s, 0, 0)), _const_spec((8, 1024)),
                  _const_spec((256, 256)), _const_spec((1, 256))],
        out_specs=pl.BlockSpec((1, A_KV_GROUPS, N_CMP, 128), lambda s: (s, 0, 0, 0)),
        out_shape=jax.ShapeDtypeStruct((nseq, A_KV_GROUPS, N_CMP, 128), F32),
        scratch_shapes=[pltpu.VMEM((A_KV_GROUPS, N_CMP, 128), F32)],
        compiler_params=_cp("parallel"),
        name="cmp_fin",
    )(proj, bias, bd4, kg)


def _slopes(g, shape, axis):
    r = lax.broadcasted_iota(jnp.int32, shape, axis)
    return jnp.exp2(-0.5 * (g * A_GROUP_SIZE + r + 1).astype(F32))


def _cmp_attn_body(q_ref, kcv_ref, oc_ref, sel_ref, *rest, qb, t0, n_pick, emit_bias):
    flag_ref = rest[0] if emit_bias else None
    g = pl.program_id(1)
    i = pl.program_id(2)
    R = A_GROUP_SIZE
    q = q_ref[0, 0].reshape(R * qb, 128)
    t = t0 + i * qb + lax.broadcasted_iota(jnp.int32, (1, qb, 1), 1)
    blk = lax.broadcasted_iota(jnp.int32, (1, 1, N_BLK), 2)
    slope = _slopes(g, (R, 1, 1), 0)
    logits, valids, kvs = [], [], []
    mx = jnp.full((R, qb, 1), NEG_INF, F32)
    for j in range(4):
        kv = kcv_ref[0, 0, j * N_BLK:(j + 1) * N_BLK, :].astype(BF16)
        cpos = SLC_BLOCK * blk + (CMP_STRIDE * j + CMP_LEN - 1)
        valid = cpos <= t
        lg = _dot_nt(q, kv).reshape(R, qb, N_BLK) - slope * (t - cpos).astype(F32)
        lg = jnp.where(valid, lg, NEG_INF)
        mx = jnp.maximum(mx, jnp.max(lg, axis=-1, keepdims=True))
        logits.append(lg)
        valids.append(valid)
        kvs.append(kv)
    ps = [jnp.where(v, jnp.exp(lg - mx), 0.0) for lg, v in zip(logits, valids)]
    den = sum(jnp.sum(p, axis=-1, keepdims=True) for p in ps) + TINY
    acc = jnp.zeros((R * qb, 128), F32)
    imp = jnp.zeros((qb, N_BLK), F32)
    for j in range(4):
        pc = ps[j] / den
        acc = acc + _dot(pc.reshape(R * qb, N_BLK).astype(BF16), kvs[j])
        imp = imp + jnp.sum(pc, axis=0)
    acc = acc.reshape(R, qb, 128)
    oc_ref[0] = jnp.concatenate([acc[r, :, 64:] for r in range(R)], axis=-1)

    square = qb == N_BLK
    axis = 0 if square else 1
    if square:
        imp = imp.T
        tq = t0 + i * qb + lax.broadcasted_iota(jnp.int32, (1, qb), 1)
        blk2 = lax.broadcasted_iota(jnp.int32, (N_BLK, 1), 0)
    else:
        tq = t[0]
        blk2 = blk[0]
    blkf = blk2.astype(F32)
    jt = tq // SLC_BLOCK
    forced = (blk2 == 0) | (blk2 == jt) | (blk2 == jt - 1)
    score = jnp.where(blk2 > jt, NEG_INF, imp + jnp.where(forced, FORCE_BONUS, 0.0))
    sel = jnp.zeros(score.shape, F32)
    for _ in range(n_pick):
        smax = jnp.max(score, axis=axis, keepdims=True)
        first = jnp.min(jnp.where(score == smax, blkf, float(N_BLK)), axis=axis, keepdims=True)
        pick = blkf == first
        sel = jnp.where(pick, 1.0, sel)
        score = jnp.where(pick, -jnp.inf, score)
    if square:
        sel = sel.T
    if emit_bias:
        sel_ref[0, 0] = ((sel - 1.0) * MASK_BIG).astype(BF16)
        any_blk = jnp.broadcast_to(jnp.max(sel, axis=0, keepdims=True), (SUBLANE, N_BLK))
        grp = (lax.broadcasted_iota(jnp.int32, (N_BLK, N_BLK), 0) // (SLC_TK // SLC_BLOCK)
               == lax.broadcasted_iota(jnp.int32, (N_BLK, N_BLK), 1))
        flag_ref[0, 0, 0] = _dot(any_blk.astype(BF16), jnp.where(grp, 1.0, 0.0).astype(BF16))
    else:
        sel_ref[0, 0] = sel


def _cmp_attn(qe, kcv, *, qb, t0, n_pick, emit_bias):
    ns, G, R, T, _ = qe.shape
    nqb = T // qb
    out_specs = [pl.BlockSpec((1, qb, 256), lambda b, g, i: (b, i, g)),
                 pl.BlockSpec((1, 1, qb, N_BLK), lambda b, g, i: (b, g, i, 0))]
    out_shape = [jax.ShapeDtypeStruct((ns, T, 1024), F32),
                 jax.ShapeDtypeStruct((ns, G, T, N_BLK), BF16 if emit_bias else F32)]
    if emit_bias:
        out_specs.append(pl.BlockSpec((1, 1, 1, SUBLANE, N_BLK), lambda b, g, i: (b, g, i, 0, 0)))
        out_shape.append(jax.ShapeDtypeStruct((ns, G, nqb, SUBLANE, N_BLK), F32))
    return pl.pallas_call(
        functools.partial(_cmp_attn_body, qb=qb, t0=t0, n_pick=n_pick, emit_bias=emit_bias),
        grid=(ns, G, nqb),
        in_specs=[pl.BlockSpec((1, 1, R, qb, 128), lambda b, g, i: (b, g, 0, i, 0)),
                  pl.BlockSpec((1, 1, N_CMP, 128), lambda b, g, i: (b, g, 0, 0))],
        out_specs=out_specs,
        out_shape=out_shape,
        compiler_params=_cp("parallel", "parallel", "parallel"),
        name="cmp_attn",
    )(qe, kcv)


SLC_TK = 256
WIN_TK = 128
QB = 128


N_KTILE = N_BLK * SLC_BLOCK // SLC_TK
SLC_GROUP = 4


def _p_attn_body(flag_ref, qt_ref, kxs_ref, vts_ref, kxw_ref, vtw_ref, os_ref, ow_ref,
                 m_sc, acc_sc, lst):
    b = pl.program_id(0)
    g = pl.program_id(1)
    i = pl.program_id(2)
    R = A_GROUP_SIZE
    rows = R * QB
    qt = qt_ref[0, 0, 0]
    t = i * QB + lax.broadcasted_iota(jnp.int32, (1, rows), 1) % QB

    def init():
        m_sc[...] = jnp.full((1, rows), NEG_INF, F32)
        acc_sc[...] = jnp.zeros((128, rows), F32)

    def step(qq, kx, vt, k0, keep):
        s = _dot(kx, qq)
        if keep is not None:
            kpos = k0 + lax.broadcasted_iota(jnp.int32, (kx.shape[0], 1), 0)
            s = jnp.where(keep(t - kpos), s, -MASK_BIG)
        m_old = m_sc[...]
        m_new = jnp.maximum(m_old, jnp.max(s, axis=0, keepdims=True))
        p = jnp.exp(s - m_new)
        acc_sc[...] = jnp.exp(m_old - m_new) * acc_sc[...] + _dot(vt, p.astype(BF16))
        m_sc[...] = m_new

    def finish(o_ref):
        acc = acc_sc[...]
        o = (acc / (acc[64:65, :] + TINY)).T[:, :64].reshape(R, QB, 64)
        o_ref[0] = jnp.concatenate([o[r] for r in range(R)], axis=-1)

    def slc_tiles(kts):
        k0s = [pl.multiple_of(kt * SLC_TK, SLC_TK) for kt in kts]
        kx = jnp.concatenate([kxs_ref[0, 0, pl.ds(k0, SLC_TK), :] for k0 in k0s], axis=0)
        vt = jnp.concatenate([vts_ref[0, 0, :, pl.ds(k0, SLC_TK)] for k0 in k0s], axis=1)
        return kx, vt

    init()
    fbase = ((b * pl.num_programs(1) + g) * pl.num_programs(2) + i) * N_KTILE
    diag = i * QB // SLC_TK

    def list_tile(kt, n):
        hit = flag_ref[fbase + kt] > 0

        @pl.when(hit)
        def _():
            lst[n] = kt
        return n + hit.astype(jnp.int32)

    n_hit = lax.fori_loop(0, diag, list_tile, 0)

    def visit(first, count):
        kx, vt = slc_tiles([lst[first + u] for u in range(count)])
        step(qt, kx, vt, 0, None)

    def slc_group(j, carry):
        visit(j * SLC_GROUP, SLC_GROUP)
        return carry

    lax.fori_loop(0, n_hit // SLC_GROUP, slc_group, 0)
    done = n_hit // SLC_GROUP * SLC_GROUP
    size = SLC_GROUP // 2
    while size >= 1:
        take = ((n_hit - done) // size) > 0
        pl.when(take)(functools.partial(visit, done, size))
        done = done + jnp.where(take, size, 0)
        size //= 2
    kx, vt = slc_tiles([diag])
    step(qt, kx, vt, diag * SLC_TK, lambda d: d >= 0)
    finish(os_ref)

    init()
    n_band = WINDOW + QB
    k0 = pl.multiple_of(jnp.maximum(i * QB - WINDOW, 0), QB)
    step(qt[:128, :], kxw_ref[0, 0, pl.ds(k0, n_band), :], vtw_ref[0, 0, :, pl.ds(k0, n_band)], k0,
         lambda d: (d >= 0) & (d < WINDOW))
    finish(ow_ref)


def _p_attn(flags, qt, kxs, vts, kxw, vtw):
    B, G, nqb, _, rows = qt.shape
    T = nqb * QB
    k_spec = lambda n: pl.BlockSpec((1, 1, T, n), lambda b, g, i, fl: (b, g, 0, 0))
    v_spec = pl.BlockSpec((1, 1, 128, T), lambda b, g, i, fl: (b, g, 0, 0))
    o_spec = pl.BlockSpec((1, QB, 256), lambda b, g, i, fl: (b, i, g))
    grid_spec = pltpu.PrefetchScalarGridSpec(
        num_scalar_prefetch=1,
        grid=(B, G, nqb),
        in_specs=[pl.BlockSpec((1, 1, 1, 256, rows), lambda b, g, i, fl: (b, g, i, 0, 0)),
                  k_spec(256), v_spec, k_spec(128), v_spec],
        out_specs=[o_spec, o_spec],
        scratch_shapes=[pltpu.VMEM((1, rows), F32), pltpu.VMEM((128, rows), F32),
                        pltpu.SMEM((N_KTILE,), jnp.int32)],
    )
    return pl.pallas_call(
        _p_attn_body,
        grid_spec=grid_spec,
        out_shape=[jax.ShapeDtypeStruct((B, T, 1024), F32)] * 2,
        compiler_params=_cp("parallel", "parallel", "arbitrary"),
        name="p_attn",
    )(flags, qt, kxs, vts, kxw, vtw)


S_PAGES = 8
SQ = 8
S_ROWS = A_HEADS * SQ


def _s_rows_meta(past_len):
    row = lax.broadcasted_iota(jnp.int32, (S_ROWS, 1), 0)
    slope = jnp.exp2(-0.5 * (row // SQ + 1).astype(F32))
    t = past_len + row % SQ
    return slope, t


def _s_step(q, kvt, valid, bias, m, l, acc):
    kvt = kvt.astype(BF16)
    s = _dot(q, kvt[:256]) - bias
    s = jnp.where(valid, s, NEG_INF)
    mn = jnp.maximum(m, jnp.max(s, axis=-1, keepdims=True))
    alpha = jnp.exp(m - mn)
    p = jnp.where(valid, jnp.exp(s - mn), 0.0)
    l = alpha * l + jnp.sum(p, axis=-1, keepdims=True)
    return mn, l, alpha * acc + _dot_nt(p.astype(BF16), kvt[256:])


def _s_out(l, acc):
    o = acc / (l + TINY)
    pieces = []
    for g in range(A_KV_GROUPS):
        for r in range(A_GROUP_SIZE):
            r0 = (g * A_GROUP_SIZE + r) * SQ
            pieces.append(o[r0:r0 + SQ, g * 64:(g + 1) * 64])
    return jnp.concatenate(pieces, axis=-1)


def _s_attn_body(tbl_ref, *refs, past_len):
    page_refs = refs[:S_PAGES]
    q_ref, sel_ref, new_s_ref, win_ref, new_w_ref, os_ref, ow_ref, m_sc, l_sc, acc_sc = refs[S_PAGES:]
    j = pl.program_id(1)
    nj = pl.num_programs(1)
    q = q_ref[0]
    slope, t = _s_rows_meta(past_len)
    sel = sel_ref[0]
    selr = jnp.broadcast_to(sel[:, None], (A_KV_GROUPS, A_GROUP_SIZE, SQ, N_BLK))
    selr = selr.reshape(S_ROWS, N_BLK).astype(BF16)

    @pl.when(j == 0)
    def _():
        m_sc[...] = jnp.full((S_ROWS, 1), NEG_INF, F32)
        l_sc[...] = jnp.zeros((S_ROWS, 1), F32)
        acc_sc[...] = jnp.zeros((S_ROWS, 256), F32)

    m, l, acc = m_sc[...], l_sc[...], acc_sc[...]
    for k in range(S_PAGES):
        kv = page_refs[k][0, 0]
        k0 = (j * S_PAGES + k) * PAGE_SIZE
        kpos2 = k0 + lax.broadcasted_iota(jnp.int32, (N_BLK, PAGE_SIZE), 1)
        blk2 = lax.broadcasted_iota(jnp.int32, (N_BLK, PAGE_SIZE), 0)
        expand = jnp.where(kpos2 // SLC_BLOCK == blk2, 1.0, 0.0).astype(BF16)
        chosen = _dot(selr, expand)
        kpos = k0 + lax.broadcasted_iota(jnp.int32, (1, PAGE_SIZE), 1)
        valid = (chosen > 0.5) & (kpos <= t)
        m, l, acc = _s_step(q, kv, valid, slope * (t - kpos).astype(F32), m, l, acc)
    m_sc[...], l_sc[...], acc_sc[...] = m, l, acc

    @pl.when(j == nj - 1)
    def _():
        knew = past_len + lax.broadcasted_iota(jnp.int32, (1, PAGE_SIZE), 1)
        dnew = t - knew
        _, l2, acc2 = _s_step(q, new_s_ref[0], dnew >= 0, slope * dnew.astype(F32), m, l, acc)
        os_ref[0] = _s_out(l2, acc2)
        nbuf = win_ref.shape[3]
        wpos = (past_len - nbuf) + lax.broadcasted_iota(jnp.int32, (1, nbuf), 1)
        dw = t - wpos
        mw = jnp.full((S_ROWS, 1), NEG_INF, F32)
        lw = jnp.zeros((S_ROWS, 1), F32)
        aw = jnp.zeros((S_ROWS, 256), F32)
        mw, lw, aw = _s_step(q, win_ref[0, 0], (dw >= 0) & (dw < WINDOW),
                             slope * dw.astype(F32), mw, lw, aw)
        _, lw, aw = _s_step(q, new_w_ref[0], (dnew >= 0) & (dnew < WINDOW),
                            slope * dnew.astype(F32), mw, lw, aw)
        ow_ref[0] = _s_out(lw, aw)


def _s_attn(table, pool, qx, sel, new_s, win, new_w, *, layer, past_len):
    ns, npg = table.shape
    steps = npg // S_PAGES

    def page_spec(k):
        return pl.BlockSpec((1, 1, 512, PAGE_SIZE),
                            lambda s, j, tbl: (layer, tbl[s, j * S_PAGES + k], 0, 0))

    per_seq = lambda shape: pl.BlockSpec((1,) + shape, lambda s, j, tbl: (s,) + (0,) * len(shape))
    grid_spec = pltpu.PrefetchScalarGridSpec(
        num_scalar_prefetch=1,
        grid=(ns, steps),
        in_specs=[page_spec(k) for k in range(S_PAGES)] + [
            per_seq((S_ROWS, 256)), per_seq((A_KV_GROUPS, SQ, N_BLK)), per_seq((512, PAGE_SIZE)),
            pl.BlockSpec((1, 1, 512, win.shape[3]), lambda s, j, tbl: (layer, s, 0, 0)),
            per_seq((512, PAGE_SIZE))],
        out_specs=[per_seq((SQ, 1024)), per_seq((SQ, 1024))],
        scratch_shapes=[pltpu.VMEM((S_ROWS, 1), F32), pltpu.VMEM((S_ROWS, 1), F32),
                        pltpu.VMEM((S_ROWS, 256), F32)],
    )
    return pl.pallas_call(
        functools.partial(_s_attn_body, past_len=past_len),
        grid_spec=grid_spec,
        out_shape=[jax.ShapeDtypeStruct((ns, SQ, 1024), F32)] * 2,
        compiler_params=_cp("parallel", "arbitrary"),
        name="s_attn",
    )(table, *([pool] * S_PAGES), qx, sel, new_s, win, new_w)


def _gdn_body(qkv_ref, ba_ref, h0_ref, s0_ref, cw_ref, alog_ref, dt_ref, og_ref,
              o_ref, st_ref, buf, S, *, C, t_valid):
    i = pl.program_id(1)
    n_i = pl.num_programs(1)

    @pl.when(i == 0)
    def _():
        buf[0:8, :] = h0_ref[0]
        S[...] = s0_ref[0]

    buf[8:8 + C, :] = qkv_ref[0]
    y = jnp.zeros((C, 3 * DN_WIDTH), F32)
    for j in range(DN_CONV):
        y = y + cw_ref[j:j + 1, :] * buf[pl.ds(8 - (DN_CONV - 1) + j, C), :]
    y = _silu(y)
    if C >= 8:
        buf[0:8, :] = buf[C:C + 8, :]

    ba = ba_ref[0]
    sp_in = ba + dt_ref[...]
    softplus = jnp.maximum(sp_in, 0.0) + jnp.log(1.0 + jnp.exp(-jnp.abs(sp_in)))
    g_all = -jnp.exp(alog_ref[...]) * softplus
    beta_all = _sigmoid(ba)
    if t_valid < C:
        live = lax.broadcasted_iota(jnp.int32, (C, 1), 0) < t_valid
        g_all = jnp.where(live, g_all, 0.0)
        beta_all = jnp.where(live, beta_all, 0.0)
    row_c = lax.broadcasted_iota(jnp.int32, (C, C), 0)
    col_c = lax.broadcasted_iota(jnp.int32, (C, C), 1)
    gc_all = jnp.dot(jnp.where(row_c >= col_c, 1.0, 0.0), g_all, precision=HI,
                     preferred_element_type=F32)
    gc_t = gc_all.T

    GC = DN_GROUP * C
    row = lax.broadcasted_iota(jnp.int32, (GC, GC), 0)
    col = lax.broadcasted_iota(jnp.int32, (GC, GC), 1)
    same = (row // C) == (col // C)
    tril = same & (row >= col)
    stril = same & (row > col)
    eye = jnp.where(row == col, 1.0, 0.0)
    stack = lambda parts: jnp.concatenate(parts, axis=0)

    outs = []
    for h0 in range(0, DN_HEADS, DN_GROUP):
        heads = range(h0, h0 + DN_GROUP)
        qs, ks, vs = [], [], []
        for h in heads:
            qh = y[:, h * 128:(h + 1) * 128]
            kh = y[:, DN_WIDTH + h * 128:DN_WIDTH + (h + 1) * 128]
            qs.append(qh * lax.rsqrt(jnp.sum(qh * qh, axis=-1, keepdims=True) + NORM_EPS)
                      * (DN_HEAD_DIM ** -0.5))
            ks.append(kh * lax.rsqrt(jnp.sum(kh * kh, axis=-1, keepdims=True) + NORM_EPS))
            vs.append(y[:, 2 * DN_WIDTH + h * 128:2 * DN_WIDTH + (h + 1) * 128])
        q_st, k_st, v_st = stack(qs), stack(ks), stack(vs)
        b_st = stack([beta_all[:, h:h + 1] for h in heads])
        gc = stack([gc_all[:, 8 + h:9 + h] for h in heads])
        gc_row = jnp.concatenate([gc_t[8 + h:9 + h, :] for h in heads], axis=1)
        g_last = stack([jnp.broadcast_to(gc_all[C - 1:C, 8 + h:9 + h], (C, 1)) for h in heads])
        eg = jnp.exp(gc)
        decay = jnp.where(tril, jnp.exp(jnp.where(tril, gc - gc_row, 0.0)), 0.0)
        kb = k_st * b_st
        k_b = k_st.astype(BF16)
        a_mat = jnp.where(stril, _dot_nt(kb.astype(BF16), k_b) * decay, 0.0)
        pw = -a_mat
        t_mat = eye + pw
        for _ in range(max(int(math.log2(C)) - 1, 0)):
            pw_b = pw.astype(BF16)
            pw = _dot(pw_b, pw_b)
            t_mat = t_mat + _dot(t_mat.astype(BF16), pw.astype(BF16))
        t_b = t_mat.astype(BF16)
        u = _dot(t_b, (v_st * b_st).astype(BF16))
        w_b = _dot(t_b, (kb * eg).astype(BF16)).astype(BF16)
        qk = jnp.where(tril, _dot_nt(q_st.astype(BF16), k_b) * decay, 0.0)
        qe_b = (q_st * eg).astype(BF16)
        k_dec = k_st * jnp.exp(g_last - gc)
        s_b = [S[h].astype(BF16) for h in heads]
        rows = lambda a, n: a[n * C:(n + 1) * C]
        v_new = u - stack([_dot(rows(w_b, n), s_b[n]) for n in range(DN_GROUP)])
        v_b = v_new.astype(BF16)
        o_st = stack([_dot(rows(qe_b, n), s_b[n]) for n in range(DN_GROUP)]) + _dot(qk.astype(BF16), v_b)
        for n, h in enumerate(heads):
            S[h] = (S[h] * jnp.exp(gc_all[C - 1:C, 8 + h:9 + h])
                    + _dot(rows(k_dec, n).T.astype(BF16), rows(v_b, n)))
            outs.append(_rms_rows(rows(o_st, n), og_ref[...]))
    o_ref[0] = jnp.concatenate(outs, axis=-1)

    @pl.when(i == n_i - 1)
    def _():
        st_ref[0] = S[...]


def _gdn_core(qkv, ba, h0, s0, cw, alog, dt, og, *, C, t_valid):
    B, T, _ = qkv.shape
    per_b = lambda shape: pl.BlockSpec((1,) + shape, lambda b, i: (b,) + (0,) * len(shape))
    tile = lambda n: pl.BlockSpec((1, C, n), lambda b, i: (b, i, 0))
    return pl.pallas_call(
        functools.partial(_gdn_body, C=C, t_valid=t_valid),
        grid=(B, T // C),
        in_specs=[tile(3 * DN_WIDTH), tile(128), per_b((8, 3 * DN_WIDTH)),
                  per_b((DN_HEADS, 128, 128)), _const_spec((8, 3 * DN_WIDTH)),
                  _const_spec((1, 128)), _const_spec((1, 128)), _const_spec((1, 128))],
        out_specs=[tile(DN_WIDTH), per_b((DN_HEADS, 128, 128))],
        out_shape=[jax.ShapeDtypeStruct((B, T, DN_WIDTH), F32),
                   jax.ShapeDtypeStruct((B, DN_HEADS, 128, 128), F32)],
        scratch_shapes=[pltpu.VMEM((8 + C, 3 * DN_WIDTH), F32),
                        pltpu.VMEM((DN_HEADS, 128, 128), F32)],
        compiler_params=_cp("parallel", "arbitrary"),
        name="gdn_core",
    )(qkv, ba, h0, s0, cw, alog, dt, og)


CONF_HIST = 32


def _conf_body(u_ref, h0_ref, cw_ref, cb_ref, lg_ref, lb_ref, c_ref, buf, *, tm, carry):
    i = pl.program_id(1)

    @pl.when(i == 0)
    def _():
        buf[0:CONF_HIST, :] = h0_ref[0]

    buf[CONF_HIST:CONF_HIST + tm, :] = u_ref[0]
    acc = jnp.zeros((tm, 1024), F32)
    for j in range(CONF_KERNEL):
        acc = acc + cw_ref[j:j + 1, :] * buf[pl.ds(CONF_HIST - (CONF_KERNEL - 1) + j, tm), :]
    c = acc + cb_ref[...]
    mu = jnp.mean(c, axis=-1, keepdims=True)
    var = jnp.mean(jnp.square(c - mu), axis=-1, keepdims=True)
    c = (c - mu) * lax.rsqrt(var + NORM_EPS) * lg_ref[...] + lb_ref[...]
    c_ref[0] = _silu(c)
    if carry:
        buf[0:CONF_HIST, :] = buf[tm:tm + CONF_HIST, :]


def _conf_conv(u, h0, cw, cb, lg, lb):
    B, T, _ = u.shape
    tm = min(256, T)
    return pl.pallas_call(
        functools.partial(_conf_body, tm=tm, carry=T > tm),
        grid=(B, T // tm),
        in_specs=[pl.BlockSpec((1, tm, 1024), lambda b, i: (b, i, 0)),
                  pl.BlockSpec((1, CONF_HIST, 1024), lambda b, i: (b, 0, 0)),
                  _const_spec((32, 1024)), _const_spec((1, 1024)), _const_spec((1, 1024)),
                  _const_spec((1, 1024))],
        out_specs=pl.BlockSpec((1, tm, 1024), lambda b, i: (b, i, 0)),
        out_shape=jax.ShapeDtypeStruct((B, T, 1024), F32),
        scratch_shapes=[pltpu.VMEM((CONF_HIST + tm, 1024), F32)],
        compiler_params=_cp("parallel", "arbitrary"),
        name="conf_conv",
    )(u, h0, cw, cb, lg, lb)


def _block_diag_ones(n, seg):
    r = jnp.arange(n) // seg
    return (r[:, None] == r[None, :]).astype(BF16)


def _nsa_weights(w_in, q_gain, k_gain, cmp_pos, cmp_w, gate_b, w_out):
    q_w, kvc_w, kvs_w, kvw_w, gl_w, z_w = jnp.split(
        w_in, [1024, 1536, 2048, 2560, 2608], axis=1)
    wcat = jnp.concatenate([q_w, kvc_w, kvs_w, kvw_w, z_w, jnp.pad(gl_w, ((0, 0), (0, 80)))],
                           axis=1).astype(BF16)
    w5 = cmp_w.reshape(2, 2, CMP_STRIDE, A_HEAD_DIM, A_HEAD_DIM)
    eye_c = jnp.eye(2, dtype=F32)
    eye_g = jnp.eye(A_KV_GROUPS, dtype=F32)
    wbig = jnp.einsum('chlde,ca,gb->labdhcge', w5, eye_c, eye_g).reshape(8192, 1024).astype(BF16)
    pe4 = cmp_pos.reshape(2, 2, CMP_STRIDE, A_HEAD_DIM)
    pe = jnp.broadcast_to(pe4.transpose(1, 2, 0, 3)[:, :, :, None, :],
                          (2, CMP_STRIDE, 2, A_KV_GROUPS, A_HEAD_DIM)).reshape(2 * CMP_STRIDE, 512)
    pe = jnp.pad(pe, ((0, PAGE_SIZE - 2 * CMP_STRIDE), (0, 0)))
    heads = jnp.arange(1024) // A_HEAD_DIM
    ex = jnp.concatenate(
        [(jnp.arange(128)[:, None] == (br * A_HEADS + heads)[None, :]) for br in range(3)],
        axis=1).astype(BF16)
    return dict(
        wcat=wcat, wbig=wbig, pe=pe, ex=ex,
        bd=_block_diag_ones(1024, A_HEAD_DIM),
        qg=(jnp.tile(q_gain, A_HEADS) * (A_HEAD_DIM ** -0.5))[None, :],
        kcg=jnp.tile(k_gain[0], A_KV_GROUPS)[None, :],
        ksg=jnp.tile(k_gain[1], A_KV_GROUPS)[None, :],
        kwg=jnp.tile(k_gain[2], A_KV_GROUPS)[None, :],
        gb=jnp.pad(gate_b, (0, 80))[None, :],
        w_out=w_out.astype(BF16))


def _split3(x):
    p1 = x.astype(BF16)
    p2 = (x - p1.astype(F32)).astype(BF16)
    p3 = (x - p1.astype(F32) - p2.astype(F32)).astype(BF16)
    return p1, p2, p3


def _alibi_query_feats():
    slope = jnp.exp2(-8.0 * jnp.arange(1, A_HEADS + 1, dtype=F32) / A_HEADS)
    s1, s2, s3 = _split3(slope)
    sc = jnp.asarray(SLC_BLOCK, BF16)
    f = jnp.stack([s1 * sc, s2 * sc, s3 * sc, s1, s2, s3], axis=-1)
    return jnp.pad(f, ((0, 0), (0, 58))).reshape(A_KV_GROUPS, A_GROUP_SIZE, 64)


def _key_pos_feats(T):
    kp = jnp.arange(T, dtype=jnp.int32)
    a = (kp // SLC_BLOCK).astype(BF16)
    b = (kp % SLC_BLOCK).astype(BF16)
    return jnp.pad(jnp.stack([a, a, a, b, b, b], axis=-1), ((0, 0), (0, 58)))


def _prompt_kv_feats(kv, B, T, with_blocks):
    kv = kv.reshape(B, T, 2, A_KV_GROUPS, A_HEAD_DIM).transpose(2, 0, 3, 1, 4).astype(BF16)
    lead = (B, A_KV_GROUPS, T)
    parts = [kv[0], jnp.broadcast_to(_key_pos_feats(T), lead + (64,))]
    if with_blocks:
        onehot = (jnp.arange(T)[:, None] // SLC_BLOCK == jnp.arange(N_BLK)[None, :]).astype(BF16)
        parts.append(jnp.broadcast_to(onehot, lead + (N_BLK,)))
    ones = jnp.pad(jnp.ones(lead + (1,), BF16), ((0, 0),) * 3 + ((0, 63),))
    return jnp.concatenate(parts, axis=-1), jnp.concatenate([kv[1], ones], axis=-1)


def _feature_major(cache):
    L, n, rows = cache.shape[:3]
    return cache.transpose(0, 1, 3, 4, 5, 2).reshape(L, n, A_KV_WIDTH, rows)


def _nsa_layer(y, g_norm, wts, B, T, *, past=None):
    M = B * T
    x2 = y.reshape(M, D_MODEL)
    qn, kvc, kvs, kvw, sz, gt = _nsa_proj(x2, g_norm, wts['wcat'], wts['bd'], wts['qg'],
                                           wts['ksg'], wts['kwg'], wts['gb'])
    bd4 = wts['bd'][:256, :256]
    q5 = qn.reshape(B, T, A_KV_GROUPS, A_GROUP_SIZE, A_HEAD_DIM).transpose(0, 2, 3, 1, 4)
    if past is None:
        n_pg = T // PAGE_SIZE
        table = jnp.arange(B * n_pg, dtype=jnp.int32).reshape(B, n_pg)
        proj, bias = _cmp_proj(kvc.reshape(B * n_pg, PAGE_SIZE, A_KV_WIDTH), table,
                               wts['wbig'], wts['pe'])
        kcv = _cmp_fin(proj, bias, bd4, wts['kcg'])
        qe = jnp.pad(q5, ((0, 0),) * 4 + ((0, 64),))
        oc, selb, flags = _cmp_attn(qe, kcv, qb=QB, t0=0, n_pick=N_SELECT, emit_bias=True)
        flags = (flags[:, :, :, 0, :N_KTILE] > 0).astype(jnp.int32).reshape(-1)
        lead = (B, A_KV_GROUPS, A_GROUP_SIZE, T)
        qx = jnp.concatenate([q5, jnp.broadcast_to(_alibi_query_feats()[None, :, :, None, :], lead + (64,)),
                              jnp.broadcast_to(selb[:, :, None], lead + (N_BLK,))], axis=-1)
        qt = qx.reshape(B, A_KV_GROUPS, A_GROUP_SIZE, T // QB, QB, 256).transpose(0, 1, 3, 5, 2, 4)
        qt = qt.reshape(B, A_KV_GROUPS, T // QB, 256, A_GROUP_SIZE * QB)
        kxs, vxs = _prompt_kv_feats(kvs, B, T, True)
        kxw, vxw = _prompt_kv_feats(kvw, B, T, False)
        os_, ow = _p_attn(flags, qt, kxs, vxs.transpose(0, 1, 3, 2), kxw, vxw.transpose(0, 1, 3, 2))
        oc, os_, ow = (o.reshape(M, 1024) for o in (oc, os_, ow))
        new_win = kvw.reshape(B, T, A_KV_WIDTH)[:, T - min(WINDOW, T):]
    else:
        cmp_t, slc_t, win_t, win_buf, layer, table, past_len = past
        proj, bias = _cmp_proj(cmp_t, table, wts['wbig'], wts['pe'], layer=layer)
        kcv = _cmp_fin(proj, bias, bd4, wts['kcg'])
        q5p = jnp.pad(q5, ((0, 0), (0, 0), (0, 0), (0, SQ - T), (0, 0)))
        qe = jnp.pad(q5p, ((0, 0),) * 4 + ((0, 64),))
        oc, sel = _cmp_attn(qe, kcv, qb=SQ, t0=past_len, n_pick=N_SELECT - 1, emit_bias=False)
        eye_g = jnp.eye(A_KV_GROUPS, dtype=BF16)
        qx = jnp.einsum('bgrqd,gh->bgrqhd', q5p, eye_g).reshape(B, S_ROWS, 256)
        new_t = lambda a: jnp.pad(a.reshape(B, T, A_KV_WIDTH).transpose(0, 2, 1),
                                  ((0, 0), (0, 0), (0, PAGE_SIZE - T)))
        os_, ow = _s_attn(table, slc_t, qx, sel, new_t(kvs), win_t, new_t(kvw),
                          layer=layer, past_len=past_len)
        oc, os_, ow = (o[:, :T].reshape(M, 1024) for o in (oc, os_, ow))
        w_rows = win_buf.reshape(B, -1, A_KV_WIDTH)
        full_w = jnp.concatenate([w_rows, kvw.reshape(B, T, A_KV_WIDTH)], axis=1)
        new_win = full_w[:, full_w.shape[1] - w_rows.shape[1]:]
    y_new = _nsa_out(x2, oc, os_, ow, gt, sz, wts['ex'], wts['w_out']).reshape(B, T, D_MODEL)
    shape6 = lambda a: a.reshape(B, -1, 2, A_KV_GROUPS, A_HEAD_DIM)
    return y_new, shape6(kvc), shape6(kvs), shape6(new_win)


def _gdn_layer(y, g_norm, w_in, conv_w, a_log, dt_bias, o_gain, w_out, B, T, conv_buf, s0):
    M = B * T
    x2 = y.reshape(M, D_MODEL)
    wcat = jnp.pad(w_in, ((0, 0), (0, 112))).astype(BF16)
    qkv, sz, ba = _gdn_proj(x2, g_norm, wcat)
    qkv3 = qkv.reshape(B, T, 3 * DN_WIDTH)
    C = DN_CHUNK if T >= DN_CHUNK else 8
    Tp = -(-T // C) * C
    pad_t = lambda a: jnp.pad(a, ((0, 0), (0, Tp - T), (0, 0)))
    h0 = jnp.pad(conv_buf, ((0, 0), (8 - (DN_CONV - 1), 0), (0, 0)))
    lane8 = lambda v: jnp.pad(v, (8, 112))[None, :]
    o, s_new = _gdn_core(pad_t(qkv3), pad_t(ba.reshape(B, T, 128)), h0, s0,
                         jnp.pad(conv_w, ((0, 8 - DN_CONV), (0, 0))), lane8(a_log), lane8(dt_bias),
                         o_gain[None, :], C=C, t_valid=min(T, C) if Tp != T else C)
    y_new = _out_proj(x2, o[:, :T].reshape(M, DN_WIDTH), sz, w_out.astype(BF16))
    x_ext_tail = jnp.concatenate([conv_buf, qkv3], axis=1)[:, T:]
    return y_new.reshape(B, T, D_MODEL), s_new, x_ext_tail


def _conf_layer(y, g_norm, w_in, conv_w, conv_b, ln_g, ln_b, w_out, B, T, conv_buf):
    M = B * T
    x2 = y.reshape(M, D_MODEL)
    u, sz = _conf_proj(x2, g_norm, w_in.astype(BF16))
    u3 = u.reshape(B, T, 1024)
    h0 = jnp.pad(conv_buf, ((0, 0), (CONF_HIST - (CONF_KERNEL - 1), 0), (0, 0)))
    c = _conf_conv(u3, h0, jnp.pad(conv_w, ((0, 32 - CONF_KERNEL), (0, 0))), conv_b[None, :],
                   ln_g[None, :], ln_b[None, :])
    y_new = _out_proj(x2, c.reshape(M, 1024), sz, w_out.astype(BF16))
    x_ext_tail = jnp.concatenate([conv_buf, u3], axis=1)[:, T:]
    return y_new.reshape(B, T, D_MODEL), x_ext_tail


def kernel(x_prompt, x_sample, cache_cmp_kv, cache_slc_kv, cache_win_kv, state_delta, state_delta_conv, state_conv, page_table, norm_g, a_w_in, a_q_gain, a_k_gain, a_cmp_pos, a_cmp_w, a_gate_b, a_w_out, b_w_in, b_conv_w, b_a_log, b_dt_bias, b_o_gain, b_w_out, c_w_in, c_conv_w, c_conv_b, c_ln_g, c_ln_b, c_w_out):
    depth = norm_g.shape[0]
    bp, tp, _ = x_prompt.shape
    bs, ts, _ = x_sample.shape
    past_len = page_table.shape[1] * PAGE_SIZE
    yp, ys = x_prompt, x_sample
    cmp_t, slc_t, win_t = (_feature_major(c) for c in (cache_cmp_kv, cache_slc_kv, cache_win_kv))
    outs = [[] for _ in range(12)]
    for i in range(depth):
        j, kind = divmod(i, N_MIXERS)
        gn = norm_g[i][None, :]
        if kind == 0:
            wts = _nsa_weights(a_w_in[j], a_q_gain[j], a_k_gain[j], a_cmp_pos[j], a_cmp_w[j],
                               a_gate_b[j], a_w_out[j])
            yp, cp_, sp_, wp_ = _nsa_layer(yp, gn, wts, bp, tp)
            ys, cs_, ss_, ws_ = _nsa_layer(
                ys, gn, wts, bs, ts,
                past=(cmp_t, slc_t, win_t, cache_win_kv[j], j, page_table, past_len))
            for lst, v in zip(outs[0:6], (cp_, cs_, sp_, ss_, wp_, ws_)):
                lst.append(v)
        elif kind == 1:
            wb = (b_w_in[j], b_conv_w[j], b_a_log[j], b_dt_bias[j], b_o_gain[j], b_w_out[j])
            yp, st_p, cv_p = _gdn_layer(yp, gn, *wb, bp, tp,
                                        jnp.zeros((bp, DN_CONV - 1, 3 * DN_WIDTH), F32),
                                        jnp.zeros((bp, DN_HEADS, DN_HEAD_DIM, DN_HEAD_DIM), F32))
            ys, st_s, cv_s = _gdn_layer(ys, gn, *wb, bs, ts, state_delta_conv[j], state_delta[j])
            for lst, v in zip(outs[6:10], (st_p, st_s, cv_p, cv_s)):
                lst.append(v)
        else:
            wc = (c_w_in[j], c_conv_w[j], c_conv_b[j], c_ln_g[j], c_ln_b[j], c_w_out[j])
            yp, cb_p = _conf_layer(yp, gn, *wc, bp, tp, jnp.zeros((bp, CONF_KERNEL - 1, 1024), F32))
            ys, cb_s = _conf_layer(ys, gn, *wc, bs, ts, state_conv[j])
            for lst, v in zip(outs[10:12], (cb_p, cb_s)):
                lst.append(v)
    return (yp, ys) + tuple(jnp.stack(lst) for lst in outs)
```

```python
import functools
import math

import jax
import jax.numpy as jnp
from jax import lax
from jax.experimental import pallas as pl
from jax.experimental.pallas import tpu as pltpu

F32 = jnp.float32
BF16 = jnp.bfloat16
HI = lax.Precision.HIGHEST

D_MODEL = 1024
N_MIXERS = 3
A_HEADS = 16
A_HEAD_DIM = 64
A_KV_GROUPS = 4
A_GROUP_SIZE = 4
A_KV_WIDTH = 2 * A_KV_GROUPS * A_HEAD_DIM
CMP_STRIDE = 16
CMP_LEN = 32
SLC_BLOCK = 64
N_SELECT = 16
WINDOW = 512
PAGE_SIZE = 128
N_CMP = 512
N_BLK = 128
DN_HEADS = 8
DN_HEAD_DIM = 128
DN_WIDTH = 1024
DN_CONV = 4
DN_CHUNK = 64
DN_GROUP = 4
CONF_KERNEL = 31

NORM_EPS = 1e-6
NEG_INF = -1e30
FORCE_BONUS = 1e4
TINY = 1e-30
MASK_BIG = 2.0 ** 40

LANE = 128
SUBLANE = 8
VMEM_LIMIT = 56 * 1024 * 1024


def _cp(*sem):
    return pltpu.CompilerParams(dimension_semantics=sem, vmem_limit_bytes=VMEM_LIMIT)


def _dot(a, b):
    return jnp.dot(a, b, preferred_element_type=F32)


def _dot_nt(a, b, precision=None):
    return lax.dot_general(a, b, (((1,), (1,)), ((), ())), precision=precision,
                           preferred_element_type=F32)


def _dot_tn(a, b, precision=None):
    return lax.dot_general(a, b, (((0,), (0,)), ((), ())), precision=precision,
                           preferred_element_type=F32)


def _sigmoid(x):
    return 1.0 / (1.0 + jnp.exp(-x))


def _silu(x):
    return x * _sigmoid(x)


def _rms_rows(x, g):
    return x * lax.rsqrt(jnp.mean(x * x, axis=-1, keepdims=True) + NORM_EPS) * g


def _seg_norm(v, bd, gain):
    ss = _dot((v * v).astype(BF16), bd)
    return v * lax.rsqrt(ss * (1.0 / A_HEAD_DIM) + NORM_EPS) * gain


def _const_spec(shape):
    nd = len(shape)
    return pl.BlockSpec(shape, lambda *_: (0,) * nd)


def _nsa_proj_body(x_ref, g_ref, w_ref, bd_ref, qg_ref, ksg_ref, kwg_ref, gb_ref,
                   q_ref, kvc_ref, kvs_ref, kvw_ref, sz_ref, gt_ref):
    h = _rms_rows(x_ref[...], g_ref[...])
    p = _dot(h.astype(BF16), w_ref[...])
    bd = bd_ref[...]
    bd4 = bd[:256, :256]
    q_ref[...] = _seg_norm(p[:, :1024], bd, qg_ref[...]).astype(BF16)
    kvc_ref[...] = p[:, 1024:1536]
    kvs_ref[:, :256] = _seg_norm(p[:, 1536:1792], bd4, ksg_ref[...])
    kvs_ref[:, 256:] = p[:, 1792:2048]
    kvw_ref[:, :256] = _seg_norm(p[:, 2048:2304], bd4, kwg_ref[...])
    kvw_ref[:, 256:] = p[:, 2304:2560]
    sz_ref[...] = _silu(p[:, 2560:3584])
    gt_ref[...] = _sigmoid(p[:, 3584:3712] + gb_ref[...])


def _nsa_proj(x, g, w, bd, qg, ksg, kwg, gb):
    m = x.shape[0]
    tm = min(256, m)
    row = lambda n: pl.BlockSpec((tm, n), lambda i: (i, 0))
    return pl.pallas_call(
        _nsa_proj_body,
        grid=(m // tm,),
        in_specs=[row(1024), _const_spec((1, 1024)), _const_spec(w.shape), _const_spec(bd.shape),
                  _const_spec((1, 1024)), _const_spec((1, 256)), _const_spec((1, 256)),
                  _const_spec((1, 128))],
        out_specs=[row(1024), row(512), row(512), row(512), row(1024), row(128)],
        out_shape=[jax.ShapeDtypeStruct((m, 1024), BF16), jax.ShapeDtypeStruct((m, 512), F32),
                   jax.ShapeDtypeStruct((m, 512), F32), jax.ShapeDtypeStruct((m, 512), F32),
                   jax.ShapeDtypeStruct((m, 1024), F32), jax.ShapeDtypeStruct((m, 128), F32)],
        compiler_params=_cp("parallel"),
        name="nsa_proj",
    )(x, g, w, bd, qg, ksg, kwg, gb)


def _gdn_proj_body(x_ref, g_ref, w_ref, qkv_ref, sz_ref, ba_ref):
    h = _rms_rows(x_ref[...], g_ref[...])
    p = _dot(h.astype(BF16), w_ref[...])
    qkv_ref[...] = p[:, :3072]
    sz_ref[...] = _silu(p[:, 3072:4096])
    ba_ref[...] = p[:, 4096:4224]


def _gdn_proj(x, g, w):
    m = x.shape[0]
    tm = min(256, m)
    row = lambda n: pl.BlockSpec((tm, n), lambda i: (i, 0))
    return pl.pallas_call(
        _gdn_proj_body,
        grid=(m // tm,),
        in_specs=[row(1024), _const_spec((1, 1024)), _const_spec(w.shape)],
        out_specs=[row(3072), row(1024), row(128)],
        out_shape=[jax.ShapeDtypeStruct((m, 3072), F32), jax.ShapeDtypeStruct((m, 1024), F32),
                   jax.ShapeDtypeStruct((m, 128), F32)],
        compiler_params=_cp("parallel"),
        name="gdn_proj",
    )(x, g, w)


def _conf_proj_body(x_ref, g_ref, w_ref, u_ref, sz_ref):
    h = _rms_rows(x_ref[...], g_ref[...])
    p = _dot(h.astype(BF16), w_ref[...])
    u_ref[...] = p[:, :1024] * _sigmoid(p[:, 1024:2048])
    sz_ref[...] = _silu(p[:, 2048:3072])


def _conf_proj(x, g, w):
    m = x.shape[0]
    tm = min(256, m)
    row = lambda n: pl.BlockSpec((tm, n), lambda i: (i, 0))
    return pl.pallas_call(
        _conf_proj_body,
        grid=(m // tm,),
        in_specs=[row(1024), _const_spec((1, 1024)), _const_spec(w.shape)],
        out_specs=[row(1024), row(1024)],
        out_shape=[jax.ShapeDtypeStruct((m, 1024), F32), jax.ShapeDtypeStruct((m, 1024), F32)],
        compiler_params=_cp("parallel"),
        name="conf_proj",
    )(x, g, w)


def _out_body(x_ref, a_ref, sz_ref, w_ref, y_ref):
    a = (a_ref[...] * sz_ref[...]).astype(BF16)
    y_ref[...] = x_ref[...] + _dot(a, w_ref[...])


def _out_proj(x, a, sz, w):
    m = x.shape[0]
    tm = min(256, m)
    row = pl.BlockSpec((tm, 1024), lambda i: (i, 0))
    return pl.pallas_call(
        _out_body,
        grid=(m // tm,),
        in_specs=[row, row, row, _const_spec(w.shape)],
        out_specs=row,
        out_shape=jax.ShapeDtypeStruct((m, 1024), F32),
        compiler_params=_cp("parallel"),
        name="out_proj",
    )(x, a, sz, w)


def _nsa_out_body(x_ref, oc_ref, os_ref, ow_ref, gt_ref, sz_ref, ex_ref, w_ref, y_ref):
    gt = gt_ref[...]
    hi = gt.astype(BF16)
    lo = (gt - hi.astype(F32)).astype(BF16)
    ge = _dot(hi, ex_ref[...]) + _dot(lo, ex_ref[...])
    o = ge[:, :1024] * oc_ref[...] + ge[:, 1024:2048] * os_ref[...] + ge[:, 2048:] * ow_ref[...]
    a = (o * sz_ref[...]).astype(BF16)
    y_ref[...] = x_ref[...] + _dot(a, w_ref[...])


def _nsa_out(x, oc, os_, ow, gt, sz, ex, w):
    m = x.shape[0]
    tm = min(256, m)
    row = lambda n: pl.BlockSpec((tm, n), lambda i: (i, 0))
    return pl.pallas_call(
        _nsa_out_body,
        grid=(m // tm,),
        in_specs=[row(1024), row(1024), row(1024), row(1024), row(128), row(1024),
                  _const_spec(ex.shape), _const_spec(w.shape)],
        out_specs=row(1024),
        out_shape=jax.ShapeDtypeStruct((m, 1024), F32),
        compiler_params=_cp("parallel"),
        name="nsa_out",
    )(x, oc, os_, ow, gt, sz, ex, w)


CMP_PAGES = 32
CMP_ROWS = CMP_PAGES * PAGE_SIZE


def _cmp_proj_body(tbl_ref, *refs, transposed):
    page_refs = refs[:CMP_PAGES]
    w_ref, pe_ref, proj_ref, bias_ref, xs = refs[CMP_PAGES:]
    for k in range(CMP_PAGES):
        for c in range(4):
            if transposed:
                blk = page_refs[k][0, 0, c * LANE:(c + 1) * LANE, :].T
            else:
                blk = page_refs[k][0, :, c * LANE:(c + 1) * LANE]
            xs[c, k * PAGE_SIZE:(k + 1) * PAGE_SIZE, :] = blk
    for c in range(4):
        xs[c, CMP_ROWS:CMP_ROWS + PAGE_SIZE, :] = pe_ref[:, c * LANE:(c + 1) * LANE]
    n_out = (CMP_ROWS + PAGE_SIZE) // CMP_STRIDE
    n_real = CMP_ROWS // CMP_STRIDE
    for kv in range(2):
        acc = jnp.zeros((n_out, 512), F32)
        for l in range(CMP_STRIDE):
            x_l = jnp.concatenate([xs[2 * kv + c, pl.ds(l, n_out, stride=CMP_STRIDE), :] for c in range(2)],
                                  axis=-1)
            acc = acc + _dot(x_l.astype(BF16), w_ref[l, kv])
        for half in range(2):
            col = half * 512 + kv * 256
            proj_ref[0, :, col:col + 256] = acc[:n_real, half * 256:(half + 1) * 256]
            bias_ref[:, col:col + 256] = acc[n_real:, half * 256:(half + 1) * 256]


def _cmp_proj(pool, table, wbig, pe, *, layer=None):
    nseq, npg = table.shape
    steps = npg // CMP_PAGES
    transposed = layer is not None

    def page_spec(k):
        if transposed:
            return pl.BlockSpec((1, 1, 512, PAGE_SIZE),
                                lambda s, j, tbl: (layer, tbl[s, j * CMP_PAGES + k], 0, 0))
        return pl.BlockSpec((1, PAGE_SIZE, 512), lambda s, j, tbl: (tbl[s, j * CMP_PAGES + k], 0, 0))

    n_out = CMP_ROWS // CMP_STRIDE
    grid_spec = pltpu.PrefetchScalarGridSpec(
        num_scalar_prefetch=1,
        grid=(nseq, steps),
        in_specs=[page_spec(k) for k in range(CMP_PAGES)] + [
            pl.BlockSpec((CMP_STRIDE, 2, 256, 512), lambda s, j, tbl: (0, 0, 0, 0),
                         pipeline_mode=pl.Buffered(1)),
            pl.BlockSpec((PAGE_SIZE, 512), lambda s, j, tbl: (0, 0))],
        out_specs=[pl.BlockSpec((1, n_out, 1024), lambda s, j, tbl: (s, j, 0)),
                   pl.BlockSpec((8, 1024), lambda s, j, tbl: (0, 0))],
        scratch_shapes=[pltpu.VMEM((4, CMP_ROWS + PAGE_SIZE, LANE), F32)],
    )
    return pl.pallas_call(
        functools.partial(_cmp_proj_body, transposed=transposed),
        grid_spec=grid_spec,
        out_shape=[jax.ShapeDtypeStruct((nseq, N_CMP, 1024), F32),
                   jax.ShapeDtypeStruct((8, 1024), F32)],
        compiler_params=_cp("arbitrary", "arbitrary"),
        name="cmp_proj",
    )(table, *([pool] * CMP_PAGES), wbig, pe)


def _cmp_fin_body(proj_ref, bias_ref, bd_ref, kg_ref, kc_ref, vc_ref, tmp):
    acc = proj_ref[0]
    nxt = pltpu.roll(acc[:, 512:], N_CMP - 1, axis=0)
    bias = bias_ref[0:1, 0:512] + bias_ref[1:2, 512:1024]
    comp = acc[:, :512] + nxt + bias
    kn = _seg_norm(comp[:, :256], bd_ref[...], kg_ref[...])
    lane = lax.broadcasted_iota(jnp.int32, (N_BLK, 64), 1)
    slab_row = lax.broadcasted_iota(jnp.int32, (N_BLK, 64), 0).astype(F32)
    zeros = jnp.zeros((N_BLK, 64), F32)
    for g in range(A_KV_GROUPS):
        tmp[g] = jnp.concatenate(
            [kn[:, g * 64:(g + 1) * 64], comp[:, 256 + g * 64:256 + (g + 1) * 64]], axis=-1)
        for j in range(4):
            slab = tmp[g, pl.ds(j, N_BLK, stride=4), :]
            pos = jnp.where(lane < 3, slab_row,
                            jnp.where(lane < 6, float(CMP_STRIDE * j + CMP_LEN - 1), 0.0))
            kc_ref[0, g, j * N_BLK:(j + 1) * N_BLK, :] = jnp.concatenate([slab[:, :64], pos], axis=-1)
            vc_ref[0, g, j * N_BLK:(j + 1) * N_BLK, :] = jnp.concatenate([slab[:, 64:], zeros], axis=-1)


def _cmp_fin(proj, bias, bd4, kg):
    nseq = proj.shape[0]
    out_spec = pl.BlockSpec((1, A_KV_GROUPS, N_CMP, 128), lambda s: (s, 0, 0, 0))
    return pl.pallas_call(
        _cmp_fin_body,
        grid=(nseq,),
        in_specs=[pl.BlockSpec((1, N_CMP, 1024), lambda s: (s, 0, 0)), _const_spec((8, 1024)),
                  _const_spec((256, 256)), _const_spec((1, 256))],
        out_specs=[out_spec, out_spec],
        out_shape=[jax.ShapeDtypeStruct((nseq, A_KV_GROUPS, N_CMP, 128), F32)] * 2,
        scratch_shapes=[pltpu.VMEM((A_KV_GROUPS, N_CMP, 128), F32)],
        compiler_params=_cp("parallel"),
        name="cmp_fin",
    )(proj, bias, bd4, kg)


def _slopes(g, shape, axis):
    r = lax.broadcasted_iota(jnp.int32, shape, axis)
    return jnp.exp2(-0.5 * (g * A_GROUP_SIZE + r + 1).astype(F32))


def _top_blocks(imp, tq, blk, n_pick, axis):
    blkf = blk.astype(F32)
    jt = tq // SLC_BLOCK
    forced = (blk == 0) | (blk == jt) | (blk == jt - 1)
    score = jnp.where(blk > jt, NEG_INF, imp + jnp.where(forced, FORCE_BONUS, 0.0))
    sel = jnp.zeros(score.shape, F32)
    for _ in range(n_pick):
        smax = jnp.max(score, axis=axis, keepdims=True)
        first = jnp.min(jnp.where(score == smax, blkf, float(N_BLK)), axis=axis, keepdims=True)
        pick = blkf == first
        sel = jnp.where(pick, 1.0, sel)
        score = jnp.where(pick, -jnp.inf, score)
    return sel


def _cmp_attn_body(q_ref, kc_ref, vc_ref, oc_ref, sel_ref, *, qb, t0, n_pick):
    g = pl.program_id(1)
    i = pl.program_id(2)
    R = A_GROUP_SIZE
    q = q_ref[0, 0].reshape(R * qb, 128)
    t = t0 + i * qb + lax.broadcasted_iota(jnp.int32, (1, qb, 1), 1)
    blk = lax.broadcasted_iota(jnp.int32, (1, 1, N_BLK), 2)
    slope = _slopes(g, (R, 1, 1), 0)
    logits, valids = [], []
    mx = jnp.full((R, qb, 1), NEG_INF, F32)
    for j in range(4):
        kc = kc_ref[0, 0, j * N_BLK:(j + 1) * N_BLK, :].astype(BF16)
        cpos = SLC_BLOCK * blk + (CMP_STRIDE * j + CMP_LEN - 1)
        valid = cpos <= t
        lg = _dot_nt(q, kc).reshape(R, qb, N_BLK) - slope * (t - cpos).astype(F32)
        lg = jnp.where(valid, lg, NEG_INF)
        mx = jnp.maximum(mx, jnp.max(lg, axis=-1, keepdims=True))
        logits.append(lg)
        valids.append(valid)
    ps = [jnp.where(v, jnp.exp(lg - mx), 0.0) for lg, v in zip(logits, valids)]
    den = sum(jnp.sum(p, axis=-1, keepdims=True) for p in ps) + TINY
    acc = jnp.zeros((R * qb, 128), F32)
    imp = jnp.zeros((qb, N_BLK), F32)
    for j in range(4):
        pc = ps[j] / den
        vc = vc_ref[0, 0, j * N_BLK:(j + 1) * N_BLK, :].astype(BF16)
        acc = acc + _dot(pc.reshape(R * qb, N_BLK).astype(BF16), vc)
        imp = imp + jnp.sum(pc, axis=0)
    acc = acc.reshape(R, qb, 128)
    oc_ref[0] = jnp.concatenate([acc[r, :, :64] for r in range(R)], axis=-1)
    sel_ref[0, 0] = _top_blocks(imp, t[0], blk[0], n_pick, 1)


def _cmp_attn(qe, kc, vc, *, qb, t0, n_pick):
    ns, G, R, T, _ = qe.shape
    kv_spec = pl.BlockSpec((1, 1, N_CMP, 128), lambda b, g, i: (b, g, 0, 0))
    return pl.pallas_call(
        functools.partial(_cmp_attn_body, qb=qb, t0=t0, n_pick=n_pick),
        grid=(ns, G, T // qb),
        in_specs=[pl.BlockSpec((1, 1, R, qb, 128), lambda b, g, i: (b, g, 0, i, 0)), kv_spec, kv_spec],
        out_specs=[pl.BlockSpec((1, qb, 256), lambda b, g, i: (b, i, g)),
                   pl.BlockSpec((1, 1, qb, N_BLK), lambda b, g, i: (b, g, i, 0))],
        out_shape=[jax.ShapeDtypeStruct((ns, T, 1024), F32),
                   jax.ShapeDtypeStruct((ns, G, T, N_BLK), F32)],
        compiler_params=_cp("parallel", "parallel", "parallel"),
        name="cmp_attn",
    )(qe, kc, vc)


def _cmp_attn_t_body(qa_ref, kc_ref, vc_ref, oc_ref, selb_ref, flag_ref):
    i = pl.program_id(2)
    R = A_GROUP_SIZE
    rows = R * QB
    qa = qa_ref[0, 0, 0]
    t = i * QB + lax.broadcasted_iota(jnp.int32, (1, rows), 1) % QB
    slab_row = lax.broadcasted_iota(jnp.int32, (N_BLK, 1), 0)
    logits, valids = [], []
    mx = jnp.full((1, rows), NEG_INF, F32)
    for j in range(4):
        kc = kc_ref[0, 0, j * N_BLK:(j + 1) * N_BLK, :].astype(BF16)
        valid = SLC_BLOCK * slab_row + (CMP_STRIDE * j + CMP_LEN - 1) <= t
        lg = jnp.where(valid, _dot(kc, qa), NEG_INF)
        mx = jnp.maximum(mx, jnp.max(lg, axis=0, keepdims=True))
        logits.append(lg)
        valids.append(valid)
    ps = [jnp.where(v, jnp.exp(lg - mx), 0.0) for lg, v in zip(logits, valids)]
    den = sum(jnp.sum(p, axis=0, keepdims=True) for p in ps) + TINY
    acc = jnp.zeros((128, rows), F32)
    imp = jnp.zeros((N_BLK, QB), F32)
    for j in range(4):
        pc = ps[j] / den
        vct = vc_ref[0, 0, j * N_BLK:(j + 1) * N_BLK, :].T.astype(BF16)
        acc = acc + _dot(vct, pc.astype(BF16))
        imp = imp + sum(pc[:, r * QB:(r + 1) * QB] for r in range(R))
    o = acc.T[:, :64].reshape(R, QB, 64)
    oc_ref[0] = jnp.concatenate([o[r] for r in range(R)], axis=-1)

    sel = _top_blocks(imp, t[:, :QB], slab_row, N_SELECT, 0)
    selb_ref[0, 0, 0] = ((sel - 1.0) * MASK_BIG).astype(BF16)
    any_blk = jnp.max(sel, axis=1, keepdims=True)
    grp = (lax.broadcasted_iota(jnp.int32, (N_BLK, N_BLK), 0) // (SLC_TK // SLC_BLOCK)
           == lax.broadcasted_iota(jnp.int32, (N_BLK, N_BLK), 1))
    cnt = jnp.sum(jnp.where(grp, any_blk, 0.0), axis=0, keepdims=True)
    flag_ref[0, 0, 0] = jnp.broadcast_to(cnt, (SUBLANE, N_BLK))


def _cmp_attn_t(qa, kc, vc):
    B, G, nqb = qa.shape[:3]
    kv_spec = pl.BlockSpec((1, 1, N_CMP, 128), lambda b, g, i: (b, g, 0, 0))
    return pl.pallas_call(
        _cmp_attn_t_body,
        grid=(B, G, nqb),
        in_specs=[pl.BlockSpec((1, 1, 1, 128, qa.shape[4]), lambda b, g, i: (b, g, i, 0, 0)),
                  kv_spec, kv_spec],
        out_specs=[pl.BlockSpec((1, QB, 256), lambda b, g, i: (b, i, g)),
                   pl.BlockSpec((1, 1, 1, N_BLK, QB), lambda b, g, i: (b, g, i, 0, 0)),
                   pl.BlockSpec((1, 1, 1, SUBLANE, N_BLK), lambda b, g, i: (b, g, i, 0, 0))],
        out_shape=[jax.ShapeDtypeStruct((B, nqb * QB, 1024), F32),
                   jax.ShapeDtypeStruct((B, G, nqb, N_BLK, QB), BF16),
                   jax.ShapeDtypeStruct((B, G, nqb, SUBLANE, N_BLK), F32)],
        compiler_params=_cp("parallel", "parallel", "parallel"),
        name="cmp_attn_t",
    )(qa, kc, vc)


SLC_TK = 256
WIN_TK = 128
QB = 128


N_KTILE = N_BLK * SLC_BLOCK // SLC_TK
SLC_GROUP = 4


def _p_attn_body(flag_ref, qa_ref, selb_ref, kxs_ref, vts_ref, kxw_ref, vtw_ref, os_ref, ow_ref,
                 m_sc, acc_sc, lst):
    b = pl.program_id(0)
    g = pl.program_id(1)
    i = pl.program_id(2)
    R = A_GROUP_SIZE
    rows = R * QB
    qa = qa_ref[0, 0, 0]
    qt = jnp.concatenate([qa, jnp.concatenate([selb_ref[0, 0, 0]] * R, axis=1)], axis=0)
    t = i * QB + lax.broadcasted_iota(jnp.int32, (1, rows), 1) % QB

    def init():
        m_sc[...] = jnp.full((1, rows), NEG_INF, F32)
        acc_sc[...] = jnp.zeros((128, rows), F32)

    def step(qq, kx, vt, k0, keep):
        s = _dot(kx, qq)
        if keep is not None:
            kpos = k0 + lax.broadcasted_iota(jnp.int32, (kx.shape[0], 1), 0)
            s = jnp.where(keep(t - kpos), s, -MASK_BIG)
        m_old = m_sc[...]
        m_new = jnp.maximum(m_old, jnp.max(s, axis=0, keepdims=True))
        p = jnp.exp(s - m_new)
        acc_sc[...] = jnp.exp(m_old - m_new) * acc_sc[...] + _dot(vt, p.astype(BF16))
        m_sc[...] = m_new

    def finish(o_ref):
        acc = acc_sc[...]
        o = (acc / (acc[64:65, :] + TINY)).T[:, :64].reshape(R, QB, 64)
        o_ref[0] = jnp.concatenate([o[r] for r in range(R)], axis=-1)

    def slc_tiles(kts):
        k0s = [pl.multiple_of(kt * SLC_TK, SLC_TK) for kt in kts]
        kx = jnp.concatenate([kxs_ref[0, 0, pl.ds(k0, SLC_TK), :] for k0 in k0s], axis=0)
        vt = jnp.concatenate([vts_ref[0, 0, :, pl.ds(k0, SLC_TK)] for k0 in k0s], axis=1)
        return kx, vt

    init()
    fbase = ((b * pl.num_programs(1) + g) * pl.num_programs(2) + i) * N_KTILE
    diag = i * QB // SLC_TK

    def list_tile(kt, n):
        hit = flag_ref[fbase + kt] > 0

        @pl.when(hit)
        def _():
            lst[n] = kt
        return n + hit.astype(jnp.int32)

    n_hit = lax.fori_loop(0, diag, list_tile, 0)

    def visit(first, count):
        kx, vt = slc_tiles([lst[first + u] for u in range(count)])
        step(qt, kx, vt, 0, None)

    def slc_group(j, carry):
        visit(j * SLC_GROUP, SLC_GROUP)
        return carry

    lax.fori_loop(0, n_hit // SLC_GROUP, slc_group, 0)
    done = n_hit // SLC_GROUP * SLC_GROUP
    size = SLC_GROUP // 2
    while size >= 1:
        take = ((n_hit - done) // size) > 0
        pl.when(take)(functools.partial(visit, done, size))
        done = done + jnp.where(take, size, 0)
        size //= 2
    kx, vt = slc_tiles([diag])
    step(qt, kx, vt, diag * SLC_TK, lambda d: d >= 0)
    finish(os_ref)

    init()
    n_band = WINDOW + QB
    k0 = pl.multiple_of(jnp.maximum(i * QB - WINDOW, 0), QB)
    step(qa, kxw_ref[0, 0, pl.ds(k0, n_band), :], vtw_ref[0, 0, :, pl.ds(k0, n_band)], k0,
         lambda d: (d >= 0) & (d < WINDOW))
    finish(ow_ref)


def _p_attn(flags, qa, selb, kxs, vts, kxw, vtw):
    B, G, nqb, _, rows = qa.shape
    T = nqb * QB
    k_spec = lambda n: pl.BlockSpec((1, 1, T, n), lambda b, g, i, fl: (b, g, 0, 0))
    v_spec = pl.BlockSpec((1, 1, 128, T), lambda b, g, i, fl: (b, g, 0, 0))
    o_spec = pl.BlockSpec((1, QB, 256), lambda b, g, i, fl: (b, i, g))
    grid_spec = pltpu.PrefetchScalarGridSpec(
        num_scalar_prefetch=1,
        grid=(B, G, nqb),
        in_specs=[pl.BlockSpec((1, 1, 1, 128, rows), lambda b, g, i, fl: (b, g, i, 0, 0)),
                  pl.BlockSpec((1, 1, 1, N_BLK, QB), lambda b, g, i, fl: (b, g, i, 0, 0)),
                  k_spec(256), v_spec, k_spec(128), v_spec],
        out_specs=[o_spec, o_spec],
        scratch_shapes=[pltpu.VMEM((1, rows), F32), pltpu.VMEM((128, rows), F32),
                        pltpu.SMEM((N_KTILE,), jnp.int32)],
    )
    return pl.pallas_call(
        _p_attn_body,
        grid_spec=grid_spec,
        out_shape=[jax.ShapeDtypeStruct((B, T, 1024), F32)] * 2,
        compiler_params=_cp("parallel", "parallel", "arbitrary"),
        name="p_attn",
    )(flags, qa, selb, kxs, vts, kxw, vtw)


S_PAGES = 16
SQ = 8
S_ROWS = A_HEADS * SQ


def _s_rows_meta(past_len):
    row = lax.broadcasted_iota(jnp.int32, (S_ROWS, 1), 0)
    slope = jnp.exp2(-0.5 * (row // SQ + 1).astype(F32))
    t = past_len + row % SQ
    return slope, t


def _s_step(q, kvt, valid, bias, m, l, acc):
    kvt = kvt.astype(BF16)
    s = _dot(q, kvt[:256]) - bias
    s = jnp.where(valid, s, NEG_INF)
    mn = jnp.maximum(m, jnp.max(s, axis=-1, keepdims=True))
    alpha = jnp.exp(m - mn)
    p = jnp.where(valid, jnp.exp(s - mn), 0.0)
    l = alpha * l + jnp.sum(p, axis=-1, keepdims=True)
    return mn, l, alpha * acc + _dot_nt(p.astype(BF16), kvt[256:])


def _s_out(l, acc):
    o = acc / (l + TINY)
    pieces = []
    for g in range(A_KV_GROUPS):
        for r in range(A_GROUP_SIZE):
            r0 = (g * A_GROUP_SIZE + r) * SQ
            pieces.append(o[r0:r0 + SQ, g * 64:(g + 1) * 64])
    return jnp.concatenate(pieces, axis=-1)


def _s_attn_body(tbl_ref, *refs, past_len):
    page_refs = refs[:S_PAGES]
    q_ref, sel_ref, new_s_ref, win_ref, new_w_ref, os_ref, ow_ref, m_sc, l_sc, acc_sc = refs[S_PAGES:]
    j = pl.program_id(1)
    nj = pl.num_programs(1)
    q = q_ref[0]
    slope, t = _s_rows_meta(past_len)
    sel = sel_ref[0]
    selr = jnp.broadcast_to(sel[:, None], (A_KV_GROUPS, A_GROUP_SIZE, SQ, N_BLK))
    selr = selr.reshape(S_ROWS, N_BLK).astype(BF16)

    @pl.when(j == 0)
    def _():
        m_sc[...] = jnp.full((S_ROWS, 1), NEG_INF, F32)
        l_sc[...] = jnp.zeros((S_ROWS, 1), F32)
        acc_sc[...] = jnp.zeros((S_ROWS, 256), F32)

    n_keys = S_PAGES * PAGE_SIZE
    kv = jnp.concatenate([page_refs[k][0, 0] for k in range(S_PAGES)], axis=1)
    k0 = j * n_keys
    kpos2 = k0 + lax.broadcasted_iota(jnp.int32, (N_BLK, n_keys), 1)
    blk2 = lax.broadcasted_iota(jnp.int32, (N_BLK, n_keys), 0)
    expand = jnp.where(kpos2 // SLC_BLOCK == blk2, 1.0, 0.0).astype(BF16)
    chosen = _dot(selr, expand)
    kpos = k0 + lax.broadcasted_iota(jnp.int32, (1, n_keys), 1)
    valid = (chosen > 0.5) & (kpos <= t)
    m, l, acc = _s_step(q, kv, valid, slope * (t - kpos).astype(F32),
                        m_sc[...], l_sc[...], acc_sc[...])
    m_sc[...], l_sc[...], acc_sc[...] = m, l, acc

    @pl.when(j == nj - 1)
    def _():
        knew = past_len + lax.broadcasted_iota(jnp.int32, (1, PAGE_SIZE), 1)
        dnew = t - knew
        _, l2, acc2 = _s_step(q, new_s_ref[0], dnew >= 0, slope * dnew.astype(F32), m, l, acc)
        os_ref[0] = _s_out(l2, acc2)
        nbuf = win_ref.shape[3]
        wpos = (past_len - nbuf) + lax.broadcasted_iota(jnp.int32, (1, nbuf), 1)
        dw = t - wpos
        mw = jnp.full((S_ROWS, 1), NEG_INF, F32)
        lw = jnp.zeros((S_ROWS, 1), F32)
        aw = jnp.zeros((S_ROWS, 256), F32)
        mw, lw, aw = _s_step(q, win_ref[0, 0], (dw >= 0) & (dw < WINDOW),
                             slope * dw.astype(F32), mw, lw, aw)
        _, lw, aw = _s_step(q, new_w_ref[0], (dnew >= 0) & (dnew < WINDOW),
                            slope * dnew.astype(F32), mw, lw, aw)
        ow_ref[0] = _s_out(lw, aw)


def _s_attn(table, pool, qx, sel, new_s, win, new_w, *, layer, past_len):
    ns, npg = table.shape
    steps = npg // S_PAGES

    def page_spec(k):
        return pl.BlockSpec((1, 1, 512, PAGE_SIZE),
                            lambda s, j, tbl: (layer, tbl[s, j * S_PAGES + k], 0, 0))

    per_seq = lambda shape: pl.BlockSpec((1,) + shape, lambda s, j, tbl: (s,) + (0,) * len(shape))
    grid_spec = pltpu.PrefetchScalarGridSpec(
        num_scalar_prefetch=1,
        grid=(ns, steps),
        in_specs=[page_spec(k) for k in range(S_PAGES)] + [
            per_seq((S_ROWS, 256)), per_seq((A_KV_GROUPS, SQ, N_BLK)), per_seq((512, PAGE_SIZE)),
            pl.BlockSpec((1, 1, 512, win.shape[3]), lambda s, j, tbl: (layer, s, 0, 0)),
            per_seq((512, PAGE_SIZE))],
        out_specs=[per_seq((SQ, 1024)), per_seq((SQ, 1024))],
        scratch_shapes=[pltpu.VMEM((S_ROWS, 1), F32), pltpu.VMEM((S_ROWS, 1), F32),
                        pltpu.VMEM((S_ROWS, 256), F32)],
    )
    return pl.pallas_call(
        functools.partial(_s_attn_body, past_len=past_len),
        grid_spec=grid_spec,
        out_shape=[jax.ShapeDtypeStruct((ns, SQ, 1024), F32)] * 2,
        compiler_params=_cp("parallel", "arbitrary"),
        name="s_attn",
    )(table, *([pool] * S_PAGES), qx, sel, new_s, win, new_w)


def _gdn_body(qkv_ref, ba_ref, h0_ref, s0_ref, cw_ref, alog_ref, dt_ref, og_ref,
              o_ref, st_ref, buf, S, *, C, t_valid):
    i = pl.program_id(1)
    n_i = pl.num_programs(1)

    @pl.when(i == 0)
    def _():
        buf[0:8, :] = h0_ref[0]
        S[...] = s0_ref[0]

    buf[8:8 + C, :] = qkv_ref[0]
    y = jnp.zeros((C, 3 * DN_WIDTH), F32)
    for j in range(DN_CONV):
        y = y + cw_ref[j:j + 1, :] * buf[pl.ds(8 - (DN_CONV - 1) + j, C), :]
    y = _silu(y)
    if C >= 8:
        buf[0:8, :] = buf[C:C + 8, :]

    ba = ba_ref[0]
    sp_in = ba + dt_ref[...]
    softplus = jnp.maximum(sp_in, 0.0) + jnp.log(1.0 + jnp.exp(-jnp.abs(sp_in)))
    g_all = -jnp.exp(alog_ref[...]) * softplus
    beta_all = _sigmoid(ba)
    if t_valid < C:
        live = lax.broadcasted_iota(jnp.int32, (C, 1), 0) < t_valid
        g_all = jnp.where(live, g_all, 0.0)
        beta_all = jnp.where(live, beta_all, 0.0)
    row_c = lax.broadcasted_iota(jnp.int32, (C, C), 0)
    col_c = lax.broadcasted_iota(jnp.int32, (C, C), 1)
    gc_all = jnp.dot(jnp.where(row_c >= col_c, 1.0, 0.0), g_all, precision=HI,
                     preferred_element_type=F32)
    gc_t = gc_all.T

    GC = DN_GROUP * C
    row = lax.broadcasted_iota(jnp.int32, (GC, GC), 0)
    col = lax.broadcasted_iota(jnp.int32, (GC, GC), 1)
    same = (row // C) == (col // C)
    tril = same & (row >= col)
    stril = same & (row > col)
    eye = jnp.where(row == col, 1.0, 0.0)
    stack = lambda parts: jnp.concatenate(parts, axis=0)

    outs = []
    for h0 in range(0, DN_HEADS, DN_GROUP):
        heads = range(h0, h0 + DN_GROUP)
        qs, ks, vs = [], [], []
        for h in heads:
            qh = y[:, h * 128:(h + 1) * 128]
            kh = y[:, DN_WIDTH + h * 128:DN_WIDTH + (h + 1) * 128]
            qs.append(qh * lax.rsqrt(jnp.sum(qh * qh, axis=-1, keepdims=True) + NORM_EPS)
                      * (DN_HEAD_DIM ** -0.5))
            ks.append(kh * lax.rsqrt(jnp.sum(kh * kh, axis=-1, keepdims=True) + NORM_EPS))
            vs.append(y[:, 2 * DN_WIDTH + h * 128:2 * DN_WIDTH + (h + 1) * 128])
        q_st, k_st, v_st = stack(qs), stack(ks), stack(vs)
        b_st = stack([beta_all[:, h:h + 1] for h in heads])
        gc = stack([gc_all[:, 8 + h:9 + h] for h in heads])
        gc_row = jnp.concatenate([gc_t[8 + h:9 + h, :] for h in heads], axis=1)
        g_last = stack([jnp.broadcast_to(gc_all[C - 1:C, 8 + h:9 + h], (C, 1)) for h in heads])
        eg = jnp.exp(gc)
        decay = jnp.where(tril, jnp.exp(jnp.where(tril, gc - gc_row, 0.0)), 0.0)
        kb = k_st * b_st
        k_b = k_st.astype(BF16)
        a_mat = jnp.where(stril, _dot_nt(kb.astype(BF16), k_b) * decay, 0.0)
        pw = -a_mat
        t_mat = eye + pw
        for _ in range(max(int(math.log2(C)) - 1, 0)):
            pw_b = pw.astype(BF16)
            pw = _dot(pw_b, pw_b)
            t_mat = t_mat + _dot(t_mat.astype(BF16), pw.astype(BF16))
        t_b = t_mat.astype(BF16)
        u = _dot(t_b, (v_st * b_st).astype(BF16))
        w_b = _dot(t_b, (kb * eg).astype(BF16)).astype(BF16)
        qk = jnp.where(tril, _dot_nt(q_st.astype(BF16), k_b) * decay, 0.0)
        qe_b = (q_st * eg).astype(BF16)
        k_dec = k_st * jnp.exp(g_last - gc)
        s_b = [S[h].astype(BF16) for h in heads]
        rows = lambda a, n: a[n * C:(n + 1) * C]
        v_new = u - stack([_dot(rows(w_b, n), s_b[n]) for n in range(DN_GROUP)])
        v_b = v_new.astype(BF16)
        o_st = stack([_dot(rows(qe_b, n), s_b[n]) for n in range(DN_GROUP)]) + _dot(qk.astype(BF16), v_b)
        for n, h in enumerate(heads):
            S[h] = (S[h] * jnp.exp(gc_all[C - 1:C, 8 + h:9 + h])
                    + _dot(rows(k_dec, n).T.astype(BF16), rows(v_b, n)))
            outs.append(_rms_rows(rows(o_st, n), og_ref[...]))
    o_ref[0] = jnp.concatenate(outs, axis=-1)

    @pl.when(i == n_i - 1)
    def _():
        st_ref[0] = S[...]


def _gdn_core(qkv, ba, h0, s0, cw, alog, dt, og, *, C, t_valid):
    B, T, _ = qkv.shape
    per_b = lambda shape: pl.BlockSpec((1,) + shape, lambda b, i: (b,) + (0,) * len(shape))
    tile = lambda n: pl.BlockSpec((1, C, n), lambda b, i: (b, i, 0))
    return pl.pallas_call(
        functools.partial(_gdn_body, C=C, t_valid=t_valid),
        grid=(B, T // C),
        in_specs=[tile(3 * DN_WIDTH), tile(128), per_b((8, 3 * DN_WIDTH)),
                  per_b((DN_HEADS, 128, 128)), _const_spec((8, 3 * DN_WIDTH)),
                  _const_spec((1, 128)), _const_spec((1, 128)), _const_spec((1, 128))],
        out_specs=[tile(DN_WIDTH), per_b((DN_HEADS, 128, 128))],
        out_shape=[jax.ShapeDtypeStruct((B, T, DN_WIDTH), F32),
                   jax.ShapeDtypeStruct((B, DN_HEADS, 128, 128), F32)],
        scratch_shapes=[pltpu.VMEM((8 + C, 3 * DN_WIDTH), F32),
                        pltpu.VMEM((DN_HEADS, 128, 128), F32)],
        compiler_params=_cp("parallel", "arbitrary"),
        name="gdn_core",
    )(qkv, ba, h0, s0, cw, alog, dt, og)


CONF_HIST = 32


def _conf_body(u_ref, h0_ref, cw_ref, cb_ref, lg_ref, lb_ref, c_ref, buf, *, tm, carry):
    i = pl.program_id(1)

    @pl.when(i == 0)
    def _():
        buf[0:CONF_HIST, :] = h0_ref[0]

    buf[CONF_HIST:CONF_HIST + tm, :] = u_ref[0]
    acc = jnp.zeros((tm, 1024), F32)
    for j in range(CONF_KERNEL):
        acc = acc + cw_ref[j:j + 1, :] * buf[pl.ds(CONF_HIST - (CONF_KERNEL - 1) + j, tm), :]
    c = acc + cb_ref[...]
    mu = jnp.mean(c, axis=-1, keepdims=True)
    var = jnp.mean(jnp.square(c - mu), axis=-1, keepdims=True)
    c = (c - mu) * lax.rsqrt(var + NORM_EPS) * lg_ref[...] + lb_ref[...]
    c_ref[0] = _silu(c)
    if carry:
        buf[0:CONF_HIST, :] = buf[tm:tm + CONF_HIST, :]


def _conf_conv(u, h0, cw, cb, lg, lb):
    B, T, _ = u.shape
    tm = min(256, T)
    return pl.pallas_call(
        functools.partial(_conf_body, tm=tm, carry=T > tm),
        grid=(B, T // tm),
        in_specs=[pl.BlockSpec((1, tm, 1024), lambda b, i: (b, i, 0)),
                  pl.BlockSpec((1, CONF_HIST, 1024), lambda b, i: (b, 0, 0)),
                  _const_spec((32, 1024)), _const_spec((1, 1024)), _const_spec((1, 1024)),
                  _const_spec((1, 1024))],
        out_specs=pl.BlockSpec((1, tm, 1024), lambda b, i: (b, i, 0)),
        out_shape=jax.ShapeDtypeStruct((B, T, 1024), F32),
        scratch_shapes=[pltpu.VMEM((CONF_HIST + tm, 1024), F32)],
        compiler_params=_cp("parallel", "arbitrary"),
        name="conf_conv",
    )(u, h0, cw, cb, lg, lb)


def _block_diag_ones(n, seg):
    r = jnp.arange(n) // seg
    return (r[:, None] == r[None, :]).astype(BF16)


def _nsa_weights(w_in, q_gain, k_gain, cmp_pos, cmp_w, gate_b, w_out):
    q_w, kvc_w, kvs_w, kvw_w, gl_w, z_w = jnp.split(
        w_in, [1024, 1536, 2048, 2560, 2608], axis=1)
    wcat = jnp.concatenate([q_w, kvc_w, kvs_w, kvw_w, z_w, jnp.pad(gl_w, ((0, 0), (0, 80)))],
                           axis=1).astype(BF16)
    w5 = cmp_w.reshape(2, 2, CMP_STRIDE, A_HEAD_DIM, A_HEAD_DIM)
    eye_g = jnp.eye(A_KV_GROUPS, dtype=F32)
    wbig = jnp.einsum('chlde,gb->lcbdhge', w5, eye_g).reshape(CMP_STRIDE, 2, 256, 512).astype(BF16)
    pe4 = cmp_pos.reshape(2, 2, CMP_STRIDE, A_HEAD_DIM)
    pe = jnp.broadcast_to(pe4.transpose(1, 2, 0, 3)[:, :, :, None, :],
                          (2, CMP_STRIDE, 2, A_KV_GROUPS, A_HEAD_DIM)).reshape(2 * CMP_STRIDE, 512)
    pe = jnp.pad(pe, ((0, PAGE_SIZE - 2 * CMP_STRIDE), (0, 0)))
    heads = jnp.arange(1024) // A_HEAD_DIM
    ex = jnp.concatenate(
        [(jnp.arange(128)[:, None] == (br * A_HEADS + heads)[None, :]) for br in range(3)],
        axis=1).astype(BF16)
    return dict(
        wcat=wcat, wbig=wbig, pe=pe, ex=ex,
        bd=_block_diag_ones(1024, A_HEAD_DIM),
        qg=(jnp.tile(q_gain, A_HEADS) * (A_HEAD_DIM ** -0.5))[None, :],
        kcg=jnp.tile(k_gain[0], A_KV_GROUPS)[None, :],
        ksg=jnp.tile(k_gain[1], A_KV_GROUPS)[None, :],
        kwg=jnp.tile(k_gain[2], A_KV_GROUPS)[None, :],
        gb=jnp.pad(gate_b, (0, 80))[None, :],
        w_out=w_out.astype(BF16))


def _split3(x):
    p1 = x.astype(BF16)
    p2 = (x - p1.astype(F32)).astype(BF16)
    p3 = (x - p1.astype(F32) - p2.astype(F32)).astype(BF16)
    return p1, p2, p3


def _alibi_query_feats():
    slope = jnp.exp2(-8.0 * jnp.arange(1, A_HEADS + 1, dtype=F32) / A_HEADS)
    s1, s2, s3 = _split3(slope)
    sc = jnp.asarray(SLC_BLOCK, BF16)
    f = jnp.stack([s1 * sc, s2 * sc, s3 * sc, s1, s2, s3], axis=-1)
    return jnp.pad(f, ((0, 0), (0, 58))).reshape(A_KV_GROUPS, A_GROUP_SIZE, 64)


def _key_pos_feats(T):
    kp = jnp.arange(T, dtype=jnp.int32)
    a = (kp // SLC_BLOCK).astype(BF16)
    b = (kp % SLC_BLOCK).astype(BF16)
    return jnp.pad(jnp.stack([a, a, a, b, b, b], axis=-1), ((0, 0), (0, 58)))


def _prompt_kv_feats(kv, B, T, with_blocks):
    kv = kv.reshape(B, T, 2, A_KV_GROUPS, A_HEAD_DIM).transpose(2, 0, 3, 1, 4).astype(BF16)
    lead = (B, A_KV_GROUPS, T)
    parts = [kv[0], jnp.broadcast_to(_key_pos_feats(T), lead + (64,))]
    if with_blocks:
        onehot = (jnp.arange(T)[:, None] // SLC_BLOCK == jnp.arange(N_BLK)[None, :]).astype(BF16)
        parts.append(jnp.broadcast_to(onehot, lead + (N_BLK,)))
    ones = jnp.pad(jnp.ones(lead + (1,), BF16), ((0, 0),) * 3 + ((0, 63),))
    return jnp.concatenate(parts, axis=-1), jnp.concatenate([kv[1], ones], axis=-1)


def _feature_major(cache):
    L, n, rows = cache.shape[:3]
    return cache.transpose(0, 1, 3, 4, 5, 2).reshape(L, n, A_KV_WIDTH, rows)


def _nsa_layer(y, g_norm, wts, B, T, *, past=None):
    M = B * T
    x2 = y.reshape(M, D_MODEL)
    qn, kvc, kvs, kvw, sz, gt = _nsa_proj(x2, g_norm, wts['wcat'], wts['bd'], wts['qg'],
                                           wts['ksg'], wts['kwg'], wts['gb'])
    bd4 = wts['bd'][:256, :256]
    q5 = qn.reshape(B, T, A_KV_GROUPS, A_GROUP_SIZE, A_HEAD_DIM).transpose(0, 2, 3, 1, 4)
    if past is None:
        n_pg = T // PAGE_SIZE
        table = jnp.arange(B * n_pg, dtype=jnp.int32).reshape(B, n_pg)
        proj, bias = _cmp_proj(kvc.reshape(B * n_pg, PAGE_SIZE, A_KV_WIDTH), table,
                               wts['wbig'], wts['pe'])
        kc, vc = _cmp_fin(proj, bias, bd4, wts['kcg'])
        lead = (B, A_KV_GROUPS, A_GROUP_SIZE, T)
        qa = jnp.concatenate([q5, jnp.broadcast_to(_alibi_query_feats()[None, :, :, None, :], lead + (64,))],
                             axis=-1)
        qa = qa.reshape(B, A_KV_GROUPS, A_GROUP_SIZE, T // QB, QB, 128).transpose(0, 1, 3, 5, 2, 4)
        qa = qa.reshape(B, A_KV_GROUPS, T // QB, 128, A_GROUP_SIZE * QB)
        oc, selb, flags = _cmp_attn_t(qa, kc, vc)
        flags = (flags[:, :, :, 0, :N_KTILE] > 0).astype(jnp.int32).reshape(-1)
        kxs, vxs = _prompt_kv_feats(kvs, B, T, True)
        kxw, vxw = _prompt_kv_feats(kvw, B, T, False)
        os_, ow = _p_attn(flags, qa, selb, kxs, vxs.transpose(0, 1, 3, 2), kxw, vxw.transpose(0, 1, 3, 2))
        oc, os_, ow = (o.reshape(M, 1024) for o in (oc, os_, ow))
        new_win = kvw.reshape(B, T, A_KV_WIDTH)[:, T - min(WINDOW, T):]
    else:
        cmp_t, slc_t, win_t, win_buf, layer, table, past_len = past
        proj, bias = _cmp_proj(cmp_t, table, wts['wbig'], wts['pe'], layer=layer)
        kc, vc = _cmp_fin(proj, bias, bd4, wts['kcg'])
        q5p = jnp.pad(q5, ((0, 0), (0, 0), (0, 0), (0, SQ - T), (0, 0)))
        qe = jnp.pad(q5p, ((0, 0),) * 4 + ((0, 64),))
        oc, sel = _cmp_attn(qe, kc, vc, qb=SQ, t0=past_len, n_pick=N_SELECT - 1)
        eye_g = jnp.eye(A_KV_GROUPS, dtype=BF16)
        qx = jnp.einsum('bgrqd,gh->bgrqhd', q5p, eye_g).reshape(B, S_ROWS, 256)
        new_t = lambda a: jnp.pad(a.reshape(B, T, A_KV_WIDTH).transpose(0, 2, 1),
                                  ((0, 0), (0, 0), (0, PAGE_SIZE - T)))
        os_, ow = _s_attn(table, slc_t, qx, sel, new_t(kvs), win_t, new_t(kvw),
                          layer=layer, past_len=past_len)
        oc, os_, ow = (o[:, :T].reshape(M, 1024) for o in (oc, os_, ow))
        w_rows = win_buf.reshape(B, -1, A_KV_WIDTH)
        full_w = jnp.concatenate([w_rows, kvw.reshape(B, T, A_KV_WIDTH)], axis=1)
        new_win = full_w[:, full_w.shape[1] - w_rows.shape[1]:]
    y_new = _nsa_out(x2, oc, os_, ow, gt, sz, wts['ex'], wts['w_out']).reshape(B, T, D_MODEL)
    shape6 = lambda a: a.reshape(B, -1, 2, A_KV_GROUPS, A_HEAD_DIM)
    return y_new, shape6(kvc), shape6(kvs), shape6(new_win)


def _gdn_layer(y, g_norm, w_in, conv_w, a_log, dt_bias, o_gain, w_out, B, T, conv_buf, s0):
    M = B * T
    x2 = y.reshape(M, D_MODEL)
    wcat = jnp.pad(w_in, ((0, 0), (0, 112))).astype(BF16)
    qkv, sz, ba = _gdn_proj(x2, g_norm, wcat)
    qkv3 = qkv.reshape(B, T, 3 * DN_WIDTH)
    C = DN_CHUNK if T >= DN_CHUNK else 8
    Tp = -(-T // C) * C
    pad_t = lambda a: jnp.pad(a, ((0, 0), (0, Tp - T), (0, 0)))
    h0 = jnp.pad(conv_buf, ((0, 0), (8 - (DN_CONV - 1), 0), (0, 0)))
    lane8 = lambda v: jnp.pad(v, (8, 112))[None, :]
    o, s_new = _gdn_core(pad_t(qkv3), pad_t(ba.reshape(B, T, 128)), h0, s0,
                         jnp.pad(conv_w, ((0, 8 - DN_CONV), (0, 0))), lane8(a_log), lane8(dt_bias),
                         o_gain[None, :], C=C, t_valid=min(T, C) if Tp != T else C)
    y_new = _out_proj(x2, o[:, :T].reshape(M, DN_WIDTH), sz, w_out.astype(BF16))
    x_ext_tail = jnp.concatenate([conv_buf, qkv3], axis=1)[:, T:]
    return y_new.reshape(B, T, D_MODEL), s_new, x_ext_tail


def _conf_layer(y, g_norm, w_in, conv_w, conv_b, ln_g, ln_b, w_out, B, T, conv_buf):
    M = B * T
    x2 = y.reshape(M, D_MODEL)
    u, sz = _conf_proj(x2, g_norm, w_in.astype(BF16))
    u3 = u.reshape(B, T, 1024)
    h0 = jnp.pad(conv_buf, ((0, 0), (CONF_HIST - (CONF_KERNEL - 1), 0), (0, 0)))
    c = _conf_conv(u3, h0, jnp.pad(conv_w, ((0, 32 - CONF_KERNEL), (0, 0))), conv_b[None, :],
                   ln_g[None, :], ln_b[None, :])
    y_new = _out_proj(x2, c.reshape(M, 1024), sz, w_out.astype(BF16))
    x_ext_tail = jnp.concatenate([conv_buf, u3], axis=1)[:, T:]
    return y_new.reshape(B, T, D_MODEL), x_ext_tail


def kernel(x_prompt, x_sample, cache_cmp_kv, cache_slc_kv, cache_win_kv, state_delta, state_delta_conv, state_conv, page_table, norm_g, a_w_in, a_q_gain, a_k_gain, a_cmp_pos, a_cmp_w, a_gate_b, a_w_out, b_w_in, b_conv_w, b_a_log, b_dt_bias, b_o_gain, b_w_out, c_w_in, c_conv_w, c_conv_b, c_ln_g, c_ln_b, c_w_out):
    depth = norm_g.shape[0]
    bp, tp, _ = x_prompt.shape
    bs, ts, _ = x_sample.shape
    past_len = page_table.shape[1] * PAGE_SIZE
    yp, ys = x_prompt, x_sample
    cmp_t, slc_t, win_t = (_feature_major(c) for c in (cache_cmp_kv, cache_slc_kv, cache_win_kv))
    outs = [[] for _ in range(12)]
    for i in range(depth):
        j, kind = divmod(i, N_MIXERS)
        gn = norm_g[i][None, :]
        if kind == 0:
            wts = _nsa_weights(a_w_in[j], a_q_gain[j], a_k_gain[j], a_cmp_pos[j], a_cmp_w[j],
                               a_gate_b[j], a_w_out[j])
            yp, cp_, sp_, wp_ = _nsa_layer(yp, gn, wts, bp, tp)
            ys, cs_, ss_, ws_ = _nsa_layer(
                ys, gn, wts, bs, ts,
                past=(cmp_t, slc_t, win_t, cache_win_kv[j], j, page_table, past_len))
            for lst, v in zip(outs[0:6], (cp_, cs_, sp_, ss_, wp_, ws_)):
                lst.append(v)
        elif kind == 1:
            wb = (b_w_in[j], b_conv_w[j], b_a_log[j], b_dt_bias[j], b_o_gain[j], b_w_out[j])
            yp, st_p, cv_p = _gdn_layer(yp, gn, *wb, bp, tp,
                                        jnp.zeros((bp, DN_CONV - 1, 3 * DN_WIDTH), F32),
                                        jnp.zeros((bp, DN_HEADS, DN_HEAD_DIM, DN_HEAD_DIM), F32))
            ys, st_s, cv_s = _gdn_layer(ys, gn, *wb, bs, ts, state_delta_conv[j], state_delta[j])
            for lst, v in zip(outs[6:10], (st_p, st_s, cv_p, cv_s)):
                lst.append(v)
        else:
            wc = (c_w_in[j], c_conv_w[j], c_conv_b[j], c_ln_g[j], c_ln_b[j], c_w_out[j])
            yp, cb_p = _conf_layer(yp, gn, *wc, bp, tp, jnp.zeros((bp, CONF_KERNEL - 1, 1024), F32))
            ys, cb_s = _conf_layer(ys, gn, *wc, bs, ts, state_conv[j])
            for lst, v in zip(outs[10:12], (cb_p, cb_s)):
                lst.append(v)
    return (yp, ys) + tuple(jnp.stack(lst) for lst in outs)
```

```python
import functools
import math

import jax
import jax.numpy as jnp
from jax import lax
from jax.experimental import pallas as pl
from jax.experimental.pallas import tpu as pltpu

F32 = jnp.float32
BF16 = jnp.bfloat16
HI = lax.Precision.HIGHEST

D_MODEL = 1024
N_MIXERS = 3
A_HEADS = 16
A_HEAD_DIM = 64
A_KV_GROUPS = 4
A_GROUP_SIZE = 4
A_KV_WIDTH = 2 * A_KV_GROUPS * A_HEAD_DIM
CMP_STRIDE = 16
CMP_LEN = 32
SLC_BLOCK = 64
N_SELECT = 16
WINDOW = 512
PAGE_SIZE = 128
N_CMP = 512
N_BLK = 128
DN_HEADS = 8
DN_HEAD_DIM = 128
DN_WIDTH = 1024
DN_CONV = 4
DN_CHUNK = 64
DN_GROUP = 4
CONF_KERNEL = 31

NORM_EPS = 1e-6
NEG_INF = -1e30
FORCE_BONUS = 1e4
TINY = 1e-30
MASK_BIG = 2.0 ** 40

LANE = 128
SUBLANE = 8
VMEM_LIMIT = 56 * 1024 * 1024


def _cp(*sem):
    return pltpu.CompilerParams(dimension_semantics=sem, vmem_limit_bytes=VMEM_LIMIT)


def _dot(a, b):
    return jnp.dot(a, b, preferred_element_type=F32)


def _dot_nt(a, b, precision=None):
    return lax.dot_general(a, b, (((1,), (1,)), ((), ())), precision=precision,
                           preferred_element_type=F32)


def _dot_tn(a, b, precision=None):
    return lax.dot_general(a, b, (((0,), (0,)), ((), ())), precision=precision,
                           preferred_element_type=F32)


def _sigmoid(x):
    return 1.0 / (1.0 + jnp.exp(-x))


def _silu(x):
    return x * _sigmoid(x)


def _rms_rows(x, g):
    return x * lax.rsqrt(jnp.mean(x * x, axis=-1, keepdims=True) + NORM_EPS) * g


def _seg_norm(v, bd, gain):
    ss = _dot((v * v).astype(BF16), bd)
    return v * lax.rsqrt(ss * (1.0 / A_HEAD_DIM) + NORM_EPS) * gain


def _const_spec(shape):
    nd = len(shape)
    return pl.BlockSpec(shape, lambda *_: (0,) * nd)


def _nsa_proj_body(x_ref, g_ref, w_ref, bd_ref, qg_ref, ksg_ref, kwg_ref, gb_ref, *rest, qblocks):
    if qblocks:
        (al_ref, pf_ref, q_ref, kvc_ref, kvs_ref, kvw_ref, sz_ref, gt_ref,
         qa_ref, kxs_ref, vts_ref, kxw_ref, vtw_ref) = rest
    else:
        q_ref, kvc_ref, kvs_ref, kvw_ref, sz_ref, gt_ref = rest
    h = _rms_rows(x_ref[...], g_ref[...])
    p = _dot(h.astype(BF16), w_ref[...])
    bd = bd_ref[...]
    bd4 = bd[:256, :256]
    qn = _seg_norm(p[:, :1024], bd, qg_ref[...])
    q_ref[...] = qn.astype(BF16)
    for qb in range(qblocks):
        for g in range(A_KV_GROUPS):
            cols = []
            for r in range(A_GROUP_SIZE):
                hd = g * A_GROUP_SIZE + r
                feats = jnp.concatenate(
                    [qn[qb * QB:(qb + 1) * QB, hd * 64:(hd + 1) * 64],
                     jnp.broadcast_to(al_ref[hd:hd + 1, :], (QB, 64))], axis=-1)
                cols.append(feats.T)
            qa_ref[0, g, qb] = jnp.concatenate(cols, axis=1).astype(BF16)
    kvc_ref[...] = p[:, 1024:1536]
    ks = _seg_norm(p[:, 1536:1792], bd4, ksg_ref[...])
    kw = _seg_norm(p[:, 2048:2304], bd4, kwg_ref[...])
    kvs_ref[:, :256] = ks
    kvs_ref[:, 256:] = p[:, 1792:2048]
    kvw_ref[:, :256] = kw
    kvw_ref[:, 256:] = p[:, 2304:2560]
    if qblocks:
        pf = pf_ref[...]
        tm = pf.shape[0]
        one_col = jnp.where(lax.broadcasted_iota(jnp.int32, (tm, 64), 1) == 0, 1.0, 0.0)
        for g in range(A_KV_GROUPS):
            sl = slice(g * 64, (g + 1) * 64)
            kxs_ref[0, g] = jnp.concatenate([ks[:, sl], pf], axis=-1).astype(BF16)
            kxw_ref[0, g] = jnp.concatenate([kw[:, sl], pf[:, :64]], axis=-1).astype(BF16)
            vs = p[:, 1792 + g * 64:1792 + (g + 1) * 64]
            vw = p[:, 2304 + g * 64:2304 + (g + 1) * 64]
            vts_ref[0, g] = jnp.concatenate([vs, one_col], axis=-1).T.astype(BF16)
            vtw_ref[0, g] = jnp.concatenate([vw, one_col], axis=-1).T.astype(BF16)
    sz_ref[...] = _silu(p[:, 2560:3584])
    gt_ref[...] = _sigmoid(p[:, 3584:3712] + gb_ref[...])


def _nsa_proj(x, g, w, bd, qg, ksg, kwg, gb, alibi=None, pos_feats=None, seq_len=None):
    m = x.shape[0]
    tm = min(256, m)
    row = lambda n: pl.BlockSpec((tm, n), lambda i: (i, 0))
    in_specs = [row(1024), _const_spec((1, 1024)), _const_spec(w.shape), _const_spec(bd.shape),
                _const_spec((1, 1024)), _const_spec((1, 256)), _const_spec((1, 256)),
                _const_spec((1, 128))]
    out_specs = [row(1024), row(512), row(512), row(512), row(1024), row(128)]
    out_shape = [jax.ShapeDtypeStruct((m, 1024), BF16), jax.ShapeDtypeStruct((m, 512), F32),
                 jax.ShapeDtypeStruct((m, 512), F32), jax.ShapeDtypeStruct((m, 512), F32),
                 jax.ShapeDtypeStruct((m, 1024), F32), jax.ShapeDtypeStruct((m, 128), F32)]
    args = [x, g, w, bd, qg, ksg, kwg, gb]
    qblocks = 0
    if alibi is not None:
        qblocks = tm // QB
        per_seq = seq_len // tm
        nb = m // seq_len
        in_specs += [_const_spec(alibi.shape), pl.BlockSpec((tm, 192), lambda i: (i % per_seq, 0))]
        args += [alibi, pos_feats]
        key_rows = lambda n: pl.BlockSpec((1, A_KV_GROUPS, tm, n), lambda i: (i // per_seq, 0, i % per_seq, 0))
        val_cols = pl.BlockSpec((1, A_KV_GROUPS, 128, tm), lambda i: (i // per_seq, 0, 0, i % per_seq))
        out_specs += [pl.BlockSpec((1, A_KV_GROUPS, qblocks, 128, A_GROUP_SIZE * QB),
                                   lambda i: (i // per_seq, 0, i % per_seq, 0, 0)),
                      key_rows(256), val_cols, key_rows(128), val_cols]
        out_shape += [jax.ShapeDtypeStruct((nb, A_KV_GROUPS, seq_len // QB, 128, A_GROUP_SIZE * QB), BF16),
                      jax.ShapeDtypeStruct((nb, A_KV_GROUPS, seq_len, 256), BF16),
                      jax.ShapeDtypeStruct((nb, A_KV_GROUPS, 128, seq_len), BF16),
                      jax.ShapeDtypeStruct((nb, A_KV_GROUPS, seq_len, 128), BF16),
                      jax.ShapeDtypeStruct((nb, A_KV_GROUPS, 128, seq_len), BF16)]
    return pl.pallas_call(
        functools.partial(_nsa_proj_body, qblocks=qblocks),
        grid=(m // tm,),
        in_specs=in_specs,
        out_specs=out_specs,
        out_shape=out_shape,
        compiler_params=_cp("parallel"),
        name="nsa_proj",
    )(*args)


def _gdn_proj_body(x_ref, g_ref, w_ref, qkv_ref, sz_ref, ba_ref):
    h = _rms_rows(x_ref[...], g_ref[...])
    p = _dot(h.astype(BF16), w_ref[...])
    qkv_ref[...] = p[:, :3072]
    sz_ref[...] = _silu(p[:, 3072:4096])
    ba_ref[...] = p[:, 4096:4224]


def _gdn_proj(x, g, w):
    m = x.shape[0]
    tm = min(256, m)
    row = lambda n: pl.BlockSpec((tm, n), lambda i: (i, 0))
    return pl.pallas_call(
        _gdn_proj_body,
        grid=(m // tm,),
        in_specs=[row(1024), _const_spec((1, 1024)), _const_spec(w.shape)],
        out_specs=[row(3072), row(1024), row(128)],
        out_shape=[jax.ShapeDtypeStruct((m, 3072), F32), jax.ShapeDtypeStruct((m, 1024), F32),
                   jax.ShapeDtypeStruct((m, 128), F32)],
        compiler_params=_cp("parallel"),
        name="gdn_proj",
    )(x, g, w)


def _conf_proj_body(x_ref, g_ref, w_ref, u_ref, sz_ref):
    h = _rms_rows(x_ref[...], g_ref[...])
    p = _dot(h.astype(BF16), w_ref[...])
    u_ref[...] = p[:, :1024] * _sigmoid(p[:, 1024:2048])
    sz_ref[...] = _silu(p[:, 2048:3072])


def _conf_proj(x, g, w):
    m = x.shape[0]
    tm = min(256, m)
    row = lambda n: pl.BlockSpec((tm, n), lambda i: (i, 0))
    return pl.pallas_call(
        _conf_proj_body,
        grid=(m // tm,),
        in_specs=[row(1024), _const_spec((1, 1024)), _const_spec(w.shape)],
        out_specs=[row(1024), row(1024)],
        out_shape=[jax.ShapeDtypeStruct((m, 1024), F32), jax.ShapeDtypeStruct((m, 1024), F32)],
        compiler_params=_cp("parallel"),
        name="conf_proj",
    )(x, g, w)


def _out_body(x_ref, a_ref, sz_ref, w_ref, y_ref):
    a = (a_ref[...] * sz_ref[...]).astype(BF16)
    y_ref[...] = x_ref[...] + _dot(a, w_ref[...])


def _out_proj(x, a, sz, w):
    m = x.shape[0]
    tm = min(256, m)
    row = pl.BlockSpec((tm, 1024), lambda i: (i, 0))
    return pl.pallas_call(
        _out_body,
        grid=(m // tm,),
        in_specs=[row, row, row, _const_spec(w.shape)],
        out_specs=row,
        out_shape=jax.ShapeDtypeStruct((m, 1024), F32),
        compiler_params=_cp("parallel"),
        name="out_proj",
    )(x, a, sz, w)


def _nsa_out_body(x_ref, oc_ref, os_ref, ow_ref, gt_ref, sz_ref, ex_ref, w_ref, y_ref):
    gt = gt_ref[...]
    hi = gt.astype(BF16)
    lo = (gt - hi.astype(F32)).astype(BF16)
    ge = _dot(hi, ex_ref[...]) + _dot(lo, ex_ref[...])
    o = ge[:, :1024] * oc_ref[...] + ge[:, 1024:2048] * os_ref[...] + ge[:, 2048:] * ow_ref[...]
    a = (o * sz_ref[...]).astype(BF16)
    y_ref[...] = x_ref[...] + _dot(a, w_ref[...])


def _nsa_out(x, oc, os_, ow, gt, sz, ex, w):
    m = x.shape[0]
    tm = min(256, m)
    row = lambda n: pl.BlockSpec((tm, n), lambda i: (i, 0))
    return pl.pallas_call(
        _nsa_out_body,
        grid=(m // tm,),
        in_specs=[row(1024), row(1024), row(1024), row(1024), row(128), row(1024),
                  _const_spec(ex.shape), _const_spec(w.shape)],
        out_specs=row(1024),
        out_shape=jax.ShapeDtypeStruct((m, 1024), F32),
        compiler_params=_cp("parallel"),
        name="nsa_out",
    )(x, oc, os_, ow, gt, sz, ex, w)


CMP_PAGES = 32
CMP_ROWS = CMP_PAGES * PAGE_SIZE


def _cmp_proj_body(tbl_ref, *refs, transposed):
    page_refs = refs[:CMP_PAGES]
    w_ref, pe_ref, proj_ref, bias_ref, xs = refs[CMP_PAGES:]
    for k in range(CMP_PAGES):
        for c in range(4):
            if transposed:
                blk = page_refs[k][0, 0, c * LANE:(c + 1) * LANE, :].T
            else:
                blk = page_refs[k][0, :, c * LANE:(c + 1) * LANE]
            xs[c, k * PAGE_SIZE:(k + 1) * PAGE_SIZE, :] = blk
    for c in range(4):
        xs[c, CMP_ROWS:CMP_ROWS + PAGE_SIZE, :] = pe_ref[:, c * LANE:(c + 1) * LANE]
    n_out = (CMP_ROWS + PAGE_SIZE) // CMP_STRIDE
    n_real = CMP_ROWS // CMP_STRIDE
    for kv in range(2):
        acc = jnp.zeros((n_out, 512), F32)
        for l in range(CMP_STRIDE):
            x_l = jnp.concatenate([xs[2 * kv + c, pl.ds(l, n_out, stride=CMP_STRIDE), :] for c in range(2)],
                                  axis=-1)
            acc = acc + _dot(x_l.astype(BF16), w_ref[l, kv])
        for half in range(2):
            col = half * 512 + kv * 256
            proj_ref[0, :, col:col + 256] = acc[:n_real, half * 256:(half + 1) * 256]
            bias_ref[:, col:col + 256] = acc[n_real:, half * 256:(half + 1) * 256]


def _cmp_proj(pool, table, wbig, pe, *, layer=None):
    nseq, npg = table.shape
    steps = npg // CMP_PAGES
    transposed = layer is not None

    def page_spec(k):
        if transposed:
            return pl.BlockSpec((1, 1, 512, PAGE_SIZE),
                                lambda s, j, tbl: (layer, tbl[s, j * CMP_PAGES + k], 0, 0))
        return pl.BlockSpec((1, PAGE_SIZE, 512), lambda s, j, tbl: (tbl[s, j * CMP_PAGES + k], 0, 0))

    n_out = CMP_ROWS // CMP_STRIDE
    grid_spec = pltpu.PrefetchScalarGridSpec(
        num_scalar_prefetch=1,
        grid=(nseq, steps),
        in_specs=[page_spec(k) for k in range(CMP_PAGES)] + [
            pl.BlockSpec((CMP_STRIDE, 2, 256, 512), lambda s, j, tbl: (0, 0, 0, 0),
                         pipeline_mode=pl.Buffered(1)),
            pl.BlockSpec((PAGE_SIZE, 512), lambda s, j, tbl: (0, 0))],
        out_specs=[pl.BlockSpec((1, n_out, 1024), lambda s, j, tbl: (s, j, 0)),
                   pl.BlockSpec((8, 1024), lambda s, j, tbl: (0, 0))],
        scratch_shapes=[pltpu.VMEM((4, CMP_ROWS + PAGE_SIZE, LANE), F32)],
    )
    return pl.pallas_call(
        functools.partial(_cmp_proj_body, transposed=transposed),
        grid_spec=grid_spec,
        out_shape=[jax.ShapeDtypeStruct((nseq, N_CMP, 1024), F32),
                   jax.ShapeDtypeStruct((8, 1024), F32)],
        compiler_params=_cp("arbitrary", "arbitrary"),
        name="cmp_proj",
    )(table, *([pool] * CMP_PAGES), wbig, pe)


def _cmp_fin_body(proj_ref, bias_ref, bd_ref, kg_ref, kc_ref, vc_ref, tmp):
    acc = proj_ref[0]
    nxt = pltpu.roll(acc[:, 512:], N_CMP - 1, axis=0)
    bias = bias_ref[0:1, 0:512] + bias_ref[1:2, 512:1024]
    comp = acc[:, :512] + nxt + bias
    kn = _seg_norm(comp[:, :256], bd_ref[...], kg_ref[...])
    lane = lax.broadcasted_iota(jnp.int32, (N_BLK, 64), 1)
    slab_row = lax.broadcasted_iota(jnp.int32, (N_BLK, 64), 0).astype(F32)
    zeros = jnp.zeros((N_BLK, 64), F32)
    for g in range(A_KV_GROUPS):
        tmp[g] = jnp.concatenate(
            [kn[:, g * 64:(g + 1) * 64], comp[:, 256 + g * 64:256 + (g + 1) * 64]], axis=-1)
        for j in range(4):
            slab = tmp[g, pl.ds(j, N_BLK, stride=4), :]
            pos = jnp.where(lane < 3, slab_row,
                            jnp.where(lane < 6, float(CMP_STRIDE * j + CMP_LEN - 1), 0.0))
            kc_ref[0, g, j * N_BLK:(j + 1) * N_BLK, :] = jnp.concatenate([slab[:, :64], pos], axis=-1)
            vc_ref[0, g, j * N_BLK:(j + 1) * N_BLK, :] = jnp.concatenate([slab[:, 64:], zeros], axis=-1)


def _cmp_fin(proj, bias, bd4, kg):
    nseq = proj.shape[0]
    out_spec = pl.BlockSpec((1, A_KV_GROUPS, N_CMP, 128), lambda s: (s, 0, 0, 0))
    return pl.pallas_call(
        _cmp_fin_body,
        grid=(nseq,),
        in_specs=[pl.BlockSpec((1, N_CMP, 1024), lambda s: (s, 0, 0)), _const_spec((8, 1024)),
                  _const_spec((256, 256)), _const_spec((1, 256))],
        out_specs=[out_spec, out_spec],
        out_shape=[jax.ShapeDtypeStruct((nseq, A_KV_GROUPS, N_CMP, 128), F32)] * 2,
        scratch_shapes=[pltpu.VMEM((A_KV_GROUPS, N_CMP, 128), F32)],
        compiler_params=_cp("parallel"),
        name="cmp_fin",
    )(proj, bias, bd4, kg)


def _slopes(g, shape, axis):
    r = lax.broadcasted_iota(jnp.int32, shape, axis)
    return jnp.exp2(-0.5 * (g * A_GROUP_SIZE + r + 1).astype(F32))


def _top_blocks(imp, tq, blk, n_pick, axis):
    blkf = blk.astype(F32)
    jt = tq // SLC_BLOCK
    forced = (blk == 0) | (blk == jt) | (blk == jt - 1)
    score = jnp.where(blk > jt, NEG_INF, imp + jnp.where(forced, FORCE_BONUS, 0.0))
    sel = jnp.zeros(score.shape, F32)
    for _ in range(n_pick):
        smax = jnp.max(score, axis=axis, keepdims=True)
        first = jnp.min(jnp.where(score == smax, blkf, float(N_BLK)), axis=axis, keepdims=True)
        pick = blkf == first
        sel = jnp.where(pick, 1.0, sel)
        score = jnp.where(pick, -jnp.inf, score)
    return sel


def _cmp_attn_body(q_ref, kc_ref, vc_ref, oc_ref, sel_ref, *, qb, t0, n_pick):
    g = pl.program_id(1)
    i = pl.program_id(2)
    R = A_GROUP_SIZE
    q = q_ref[0, 0].reshape(R * qb, 128)
    t = t0 + i * qb + lax.broadcasted_iota(jnp.int32, (1, qb, 1), 1)
    blk = lax.broadcasted_iota(jnp.int32, (1, 1, N_BLK), 2)
    slope = _slopes(g, (R, 1, 1), 0)
    logits, valids = [], []
    mx = jnp.full((R, qb, 1), NEG_INF, F32)
    for j in range(4):
        kc = kc_ref[0, 0, j * N_BLK:(j + 1) * N_BLK, :].astype(BF16)
        cpos = SLC_BLOCK * blk + (CMP_STRIDE * j + CMP_LEN - 1)
        valid = cpos <= t
        lg = _dot_nt(q, kc).reshape(R, qb, N_BLK) - slope * (t - cpos).astype(F32)
        lg = jnp.where(valid, lg, NEG_INF)
        mx = jnp.maximum(mx, jnp.max(lg, axis=-1, keepdims=True))
        logits.append(lg)
        valids.append(valid)
    ps = [jnp.where(v, jnp.exp(lg - mx), 0.0) for lg, v in zip(logits, valids)]
    den = sum(jnp.sum(p, axis=-1, keepdims=True) for p in ps) + TINY
    acc = jnp.zeros((R * qb, 128), F32)
    imp = jnp.zeros((qb, N_BLK), F32)
    for j in range(4):
        pc = ps[j] / den
        vc = vc_ref[0, 0, j * N_BLK:(j + 1) * N_BLK, :].astype(BF16)
        acc = acc + _dot(pc.reshape(R * qb, N_BLK).astype(BF16), vc)
        imp = imp + jnp.sum(pc, axis=0)
    acc = acc.reshape(R, qb, 128)
    oc_ref[0] = jnp.concatenate([acc[r, :, :64] for r in range(R)], axis=-1)
    sel_ref[0, 0] = _top_blocks(imp, t[0], blk[0], n_pick, 1)


def _cmp_attn(qe, kc, vc, *, qb, t0, n_pick):
    ns, G, R, T, _ = qe.shape
    kv_spec = pl.BlockSpec((1, 1, N_CMP, 128), lambda b, g, i: (b, g, 0, 0))
    return pl.pallas_call(
        functools.partial(_cmp_attn_body, qb=qb, t0=t0, n_pick=n_pick),
        grid=(ns, G, T // qb),
        in_specs=[pl.BlockSpec((1, 1, R, qb, 128), lambda b, g, i: (b, g, 0, i, 0)), kv_spec, kv_spec],
        out_specs=[pl.BlockSpec((1, qb, 256), lambda b, g, i: (b, i, g)),
                   pl.BlockSpec((1, 1, qb, N_BLK), lambda b, g, i: (b, g, i, 0))],
        out_shape=[jax.ShapeDtypeStruct((ns, T, 1024), F32),
                   jax.ShapeDtypeStruct((ns, G, T, N_BLK), F32)],
        compiler_params=_cp("parallel", "parallel", "parallel"),
        name="cmp_attn",
    )(qe, kc, vc)


def _cmp_attn_t_body(qa_ref, kc_ref, vc_ref, oc_ref, selb_ref, flag_ref):
    i = pl.program_id(2)
    R = A_GROUP_SIZE
    rows = R * QB
    qa = qa_ref[0, 0, 0]
    t = i * QB + lax.broadcasted_iota(jnp.int32, (1, rows), 1) % QB
    slab_row = lax.broadcasted_iota(jnp.int32, (N_BLK, 1), 0)
    logits, valids = [], []
    mx = jnp.full((1, rows), NEG_INF, F32)
    for j in range(4):
        kc = kc_ref[0, 0, j * N_BLK:(j + 1) * N_BLK, :].astype(BF16)
        valid = SLC_BLOCK * slab_row + (CMP_STRIDE * j + CMP_LEN - 1) <= t
        lg = jnp.where(valid, _dot(kc, qa), NEG_INF)
        mx = jnp.maximum(mx, jnp.max(lg, axis=0, keepdims=True))
        logits.append(lg)
        valids.append(valid)
    ps = [jnp.where(v, jnp.exp(lg - mx), 0.0) for lg, v in zip(logits, valids)]
    den = sum(jnp.sum(p, axis=0, keepdims=True) for p in ps) + TINY
    acc = jnp.zeros((128, rows), F32)
    imp = jnp.zeros((N_BLK, QB), F32)
    for j in range(4):
        pc = ps[j] / den
        vct = vc_ref[0, 0, j * N_BLK:(j + 1) * N_BLK, :].T.astype(BF16)
        acc = acc + _dot(vct, pc.astype(BF16))
        imp = imp + sum(pc[:, r * QB:(r + 1) * QB] for r in range(R))
    o = acc.T[:, :64].reshape(R, QB, 64)
    oc_ref[0] = jnp.concatenate([o[r] for r in range(R)], axis=-1)

    sel = _top_blocks(imp, t[:, :QB], slab_row, N_SELECT, 0)
    selb_ref[0, 0, 0] = ((sel - 1.0) * MASK_BIG).astype(BF16)
    any_blk = jnp.max(sel, axis=1, keepdims=True)
    grp = (lax.broadcasted_iota(jnp.int32, (N_BLK, N_BLK), 0) // (SLC_TK // SLC_BLOCK)
           == lax.broadcasted_iota(jnp.int32, (N_BLK, N_BLK), 1))
    cnt = jnp.sum(jnp.where(grp, any_blk, 0.0), axis=0, keepdims=True)
    flag_ref[0, 0, 0] = jnp.broadcast_to(cnt, (SUBLANE, N_BLK))


def _cmp_attn_t(qa, kc, vc):
    B, G, nqb = qa.shape[:3]
    kv_spec = pl.BlockSpec((1, 1, N_CMP, 128), lambda b, g, i: (b, g, 0, 0))
    return pl.pallas_call(
        _cmp_attn_t_body,
        grid=(B, G, nqb),
        in_specs=[pl.BlockSpec((1, 1, 1, 128, qa.shape[4]), lambda b, g, i: (b, g, i, 0, 0)),
                  kv_spec, kv_spec],
        out_specs=[pl.BlockSpec((1, QB, 256), lambda b, g, i: (b, i, g)),
                   pl.BlockSpec((1, 1, 1, N_BLK, QB), lambda b, g, i: (b, g, i, 0, 0)),
                   pl.BlockSpec((1, 1, 1, SUBLANE, N_BLK), lambda b, g, i: (b, g, i, 0, 0))],
        out_shape=[jax.ShapeDtypeStruct((B, nqb * QB, 1024), F32),
                   jax.ShapeDtypeStruct((B, G, nqb, N_BLK, QB), BF16),
                   jax.ShapeDtypeStruct((B, G, nqb, SUBLANE, N_BLK), F32)],
        compiler_params=_cp("parallel", "parallel", "parallel"),
        name="cmp_attn_t",
    )(qa, kc, vc)


SLC_TK = 256
WIN_TK = 128
QB = 128


N_KTILE = N_BLK * SLC_BLOCK // SLC_TK
SLC_GROUP = 4


def _p_attn_body(flag_ref, qa_ref, selb_ref, kxs_ref, vts_ref, kxw_ref, vtw_ref, os_ref, ow_ref,
                 m_sc, acc_sc, lst):
    b = pl.program_id(0)
    g = pl.program_id(1)
    i = pl.program_id(2)
    R = A_GROUP_SIZE
    rows = R * QB
    qa = qa_ref[0, 0, 0]
    qt = jnp.concatenate([qa, jnp.concatenate([selb_ref[0, 0, 0]] * R, axis=1)], axis=0)
    t = i * QB + lax.broadcasted_iota(jnp.int32, (1, rows), 1) % QB

    def init():
        m_sc[...] = jnp.full((1, rows), NEG_INF, F32)
        acc_sc[...] = jnp.zeros((128, rows), F32)

    def step(qq, kx, vt, k0, keep):
        s = _dot(kx, qq)
        if keep is not None:
            kpos = k0 + lax.broadcasted_iota(jnp.int32, (kx.shape[0], 1), 0)
            s = jnp.where(keep(t - kpos), s, -MASK_BIG)
        m_old = m_sc[...]
        m_new = jnp.maximum(m_old, jnp.max(s, axis=0, keepdims=True))
        p = jnp.exp(s - m_new)
        acc_sc[...] = jnp.exp(m_old - m_new) * acc_sc[...] + _dot(vt, p.astype(BF16))
        m_sc[...] = m_new

    def finish(o_ref):
        acc = acc_sc[...]
        o = (acc / (acc[64:65, :] + TINY)).T[:, :64].reshape(R, QB, 64)
        o_ref[0] = jnp.concatenate([o[r] for r in range(R)], axis=-1)

    def slc_tiles(kts):
        k0s = [pl.multiple_of(kt * SLC_TK, SLC_TK) for kt in kts]
        kx = jnp.concatenate([kxs_ref[0, 0, pl.ds(k0, SLC_TK), :] for k0 in k0s], axis=0)
        vt = jnp.concatenate([vts_ref[0, 0, :, pl.ds(k0, SLC_TK)] for k0 in k0s], axis=1)
        return kx, vt

    init()
    fbase = ((b * pl.num_programs(1) + g) * pl.num_programs(2) + i) * N_KTILE
    diag = i * QB // SLC_TK

    def list_tile(kt, n):
        hit = flag_ref[fbase + kt] > 0

        @pl.when(hit)
        def _():
            lst[n] = kt
        return n + hit.astype(jnp.int32)

    n_hit = lax.fori_loop(0, diag, list_tile, 0)

    def visit(first, count):
        kx, vt = slc_tiles([lst[first + u] for u in range(count)])
        step(qt, kx, vt, 0, None)

    def slc_group(j, carry):
        visit(j * SLC_GROUP, SLC_GROUP)
        return carry

    lax.fori_loop(0, n_hit // SLC_GROUP, slc_group, 0)
    done = n_hit // SLC_GROUP * SLC_GROUP
    size = SLC_GROUP // 2
    while size >= 1:
        take = ((n_hit - done) // size) > 0
        pl.when(take)(functools.partial(visit, done, size))
        done = done + jnp.where(take, size, 0)
        size //= 2
    kx, vt = slc_tiles([diag])
    step(qt, kx, vt, diag * SLC_TK, lambda d: d >= 0)
    finish(os_ref)

    init()
    n_band = WINDOW + QB
    k0 = pl.multiple_of(jnp.maximum(i * QB - WINDOW, 0), QB)
    step(qa, kxw_ref[0, 0, pl.ds(k0, n_band), :], vtw_ref[0, 0, :, pl.ds(k0, n_band)], k0,
         lambda d: (d >= 0) & (d < WINDOW))
    finish(ow_ref)


def _p_attn(flags, qa, selb, kxs, vts, kxw, vtw):
    B, G, nqb, _, rows = qa.shape
    T = nqb * QB
    k_spec = lambda n: pl.BlockSpec((1, 1, T, n), lambda b, g, i, fl: (b, g, 0, 0))
    v_spec = pl.BlockSpec((1, 1, 128, T), lambda b, g, i, fl: (b, g, 0, 0))
    o_spec = pl.BlockSpec((1, QB, 256), lambda b, g, i, fl: (b, i, g))
    grid_spec = pltpu.PrefetchScalarGridSpec(
        num_scalar_prefetch=1,
        grid=(B, G, nqb),
        in_specs=[pl.BlockSpec((1, 1, 1, 128, rows), lambda b, g, i, fl: (b, g, i, 0, 0)),
                  pl.BlockSpec((1, 1, 1, N_BLK, QB), lambda b, g, i, fl: (b, g, i, 0, 0)),
                  k_spec(256), v_spec, k_spec(128), v_spec],
        out_specs=[o_spec, o_spec],
        scratch_shapes=[pltpu.VMEM((1, rows), F32), pltpu.VMEM((128, rows), F32),
                        pltpu.SMEM((N_KTILE,), jnp.int32)],
    )
    return pl.pallas_call(
        _p_attn_body,
        grid_spec=grid_spec,
        out_shape=[jax.ShapeDtypeStruct((B, T, 1024), F32)] * 2,
        compiler_params=_cp("parallel", "parallel", "arbitrary"),
        name="p_attn",
    )(flags, qa, selb, kxs, vts, kxw, vtw)


S_PAGES = 16
SQ = 8
S_ROWS = A_HEADS * SQ


def _s_rows_meta(past_len):
    row = lax.broadcasted_iota(jnp.int32, (S_ROWS, 1), 0)
    slope = jnp.exp2(-0.5 * (row // SQ + 1).astype(F32))
    t = past_len + row % SQ
    return slope, t


def _s_step(q, kvt, valid, bias, m, l, acc):
    kvt = kvt.astype(BF16)
    s = _dot(q, kvt[:256]) - bias
    s = jnp.where(valid, s, NEG_INF)
    mn = jnp.maximum(m, jnp.max(s, axis=-1, keepdims=True))
    alpha = jnp.exp(m - mn)
    p = jnp.where(valid, jnp.exp(s - mn), 0.0)
    l = alpha * l + jnp.sum(p, axis=-1, keepdims=True)
    return mn, l, alpha * acc + _dot_nt(p.astype(BF16), kvt[256:])


def _s_out(l, acc):
    o = acc / (l + TINY)
    pieces = []
    for g in range(A_KV_GROUPS):
        for r in range(A_GROUP_SIZE):
            r0 = (g * A_GROUP_SIZE + r) * SQ
            pieces.append(o[r0:r0 + SQ, g * 64:(g + 1) * 64])
    return jnp.concatenate(pieces, axis=-1)


def _s_attn_body(tbl_ref, *refs, past_len):
    page_refs = refs[:S_PAGES]
    q_ref, sel_ref, new_s_ref, win_ref, new_w_ref, os_ref, ow_ref, m_sc, l_sc, acc_sc = refs[S_PAGES:]
    j = pl.program_id(1)
    nj = pl.num_programs(1)
    q = q_ref[0]
    slope, t = _s_rows_meta(past_len)
    sel = sel_ref[0]
    selr = jnp.broadcast_to(sel[:, None], (A_KV_GROUPS, A_GROUP_SIZE, SQ, N_BLK))
    selr = selr.reshape(S_ROWS, N_BLK).astype(BF16)

    @pl.when(j == 0)
    def _():
        m_sc[...] = jnp.full((S_ROWS, 1), NEG_INF, F32)
        l_sc[...] = jnp.zeros((S_ROWS, 1), F32)
        acc_sc[...] = jnp.zeros((S_ROWS, 256), F32)

    n_keys = S_PAGES * PAGE_SIZE
    kv = jnp.concatenate([page_refs[k][0, 0] for k in range(S_PAGES)], axis=1)
    k0 = j * n_keys
    kpos2 = k0 + lax.broadcasted_iota(jnp.int32, (N_BLK, n_keys), 1)
    blk2 = lax.broadcasted_iota(jnp.int32, (N_BLK, n_keys), 0)
    expand = jnp.where(kpos2 // SLC_BLOCK == blk2, 1.0, 0.0).astype(BF16)
    chosen = _dot(selr, expand)
    kpos = k0 + lax.broadcasted_iota(jnp.int32, (1, n_keys), 1)
    valid = (chosen > 0.5) & (kpos <= t)
    m, l, acc = _s_step(q, kv, valid, slope * (t - kpos).astype(F32),
                        m_sc[...], l_sc[...], acc_sc[...])
    m_sc[...], l_sc[...], acc_sc[...] = m, l, acc

    @pl.when(j == nj - 1)
    def _():
        knew = past_len + lax.broadcasted_iota(jnp.int32, (1, PAGE_SIZE), 1)
        dnew = t - knew
        _, l2, acc2 = _s_step(q, new_s_ref[0], dnew >= 0, slope * dnew.astype(F32), m, l, acc)
        os_ref[0] = _s_out(l2, acc2)
        nbuf = win_ref.shape[3]
        wpos = (past_len - nbuf) + lax.broadcasted_iota(jnp.int32, (1, nbuf), 1)
        dw = t - wpos
        mw = jnp.full((S_ROWS, 1), NEG_INF, F32)
        lw = jnp.zeros((S_ROWS, 1), F32)
        aw = jnp.zeros((S_ROWS, 256), F32)
        mw, lw, aw = _s_step(q, win_ref[0, 0], (dw >= 0) & (dw < WINDOW),
                             slope * dw.astype(F32), mw, lw, aw)
        _, lw, aw = _s_step(q, new_w_ref[0], (dnew >= 0) & (dnew < WINDOW),
                            slope * dnew.astype(F32), mw, lw, aw)
        ow_ref[0] = _s_out(lw, aw)


def _s_attn(table, pool, qx, sel, new_s, win, new_w, *, layer, past_len):
    ns, npg = table.shape
    steps = npg // S_PAGES

    def page_spec(k):
        return pl.BlockSpec((1, 1, 512, PAGE_SIZE),
                            lambda s, j, tbl: (layer, tbl[s, j * S_PAGES + k], 0, 0))

    per_seq = lambda shape: pl.BlockSpec((1,) + shape, lambda s, j, tbl: (s,) + (0,) * len(shape))
    grid_spec = pltpu.PrefetchScalarGridSpec(
        num_scalar_prefetch=1,
        grid=(ns, steps),
        in_specs=[page_spec(k) for k in range(S_PAGES)] + [
            per_seq((S_ROWS, 256)), per_seq((A_KV_GROUPS, SQ, N_BLK)), per_seq((512, PAGE_SIZE)),
            pl.BlockSpec((1, 1, 512, win.shape[3]), lambda s, j, tbl: (layer, s, 0, 0)),
            per_seq((512, PAGE_SIZE))],
        out_specs=[per_seq((SQ, 1024)), per_seq((SQ, 1024))],
        scratch_shapes=[pltpu.VMEM((S_ROWS, 1), F32), pltpu.VMEM((S_ROWS, 1), F32),
                        pltpu.VMEM((S_ROWS, 256), F32)],
    )
    return pl.pallas_call(
        functools.partial(_s_attn_body, past_len=past_len),
        grid_spec=grid_spec,
        out_shape=[jax.ShapeDtypeStruct((ns, SQ, 1024), F32)] * 2,
        compiler_params=_cp("parallel", "arbitrary"),
        name="s_attn",
    )(table, *([pool] * S_PAGES), qx, sel, new_s, win, new_w)


def _gdn_body(qkv_ref, ba_ref, h0_ref, s0_ref, cw_ref, alog_ref, dt_ref, og_ref,
              o_ref, st_ref, buf, S, *, C, t_valid):
    i = pl.program_id(1)
    n_i = pl.num_programs(1)

    @pl.when(i == 0)
    def _():
        buf[0:8, :] = h0_ref[0]
        S[...] = s0_ref[0]

    buf[8:8 + C, :] = qkv_ref[0]
    y = jnp.zeros((C, 3 * DN_WIDTH), F32)
    for j in range(DN_CONV):
        y = y + cw_ref[j:j + 1, :] * buf[pl.ds(8 - (DN_CONV - 1) + j, C), :]
    y = _silu(y)
    if C >= 8:
        buf[0:8, :] = buf[C:C + 8, :]

    ba = ba_ref[0]
    sp_in = ba + dt_ref[...]
    softplus = jnp.maximum(sp_in, 0.0) + jnp.log(1.0 + jnp.exp(-jnp.abs(sp_in)))
    g_all = -jnp.exp(alog_ref[...]) * softplus
    beta_all = _sigmoid(ba)
    if t_valid < C:
        live = lax.broadcasted_iota(jnp.int32, (C, 1), 0) < t_valid
        g_all = jnp.where(live, g_all, 0.0)
        beta_all = jnp.where(live, beta_all, 0.0)
    row_c = lax.broadcasted_iota(jnp.int32, (C, C), 0)
    col_c = lax.broadcasted_iota(jnp.int32, (C, C), 1)
    gc_all = jnp.dot(jnp.where(row_c >= col_c, 1.0, 0.0), g_all, precision=HI,
                     preferred_element_type=F32)
    gc_t = gc_all.T

    GC = DN_GROUP * C
    row = lax.broadcasted_iota(jnp.int32, (GC, GC), 0)
    col = lax.broadcasted_iota(jnp.int32, (GC, GC), 1)
    same = (row // C) == (col // C)
    tril = same & (row >= col)
    stril = same & (row > col)
    eye = jnp.where(row == col, 1.0, 0.0)
    stack = lambda parts: jnp.concatenate(parts, axis=0)

    outs = []
    for h0 in range(0, DN_HEADS, DN_GROUP):
        heads = range(h0, h0 + DN_GROUP)
        qs, ks, vs = [], [], []
        for h in heads:
            qh = y[:, h * 128:(h + 1) * 128]
            kh = y[:, DN_WIDTH + h * 128:DN_WIDTH + (h + 1) * 128]
            qs.append(qh * lax.rsqrt(jnp.sum(qh * qh, axis=-1, keepdims=True) + NORM_EPS)
                      * (DN_HEAD_DIM ** -0.5))
            ks.append(kh * lax.rsqrt(jnp.sum(kh * kh, axis=-1, keepdims=True) + NORM_EPS))
            vs.append(y[:, 2 * DN_WIDTH + h * 128:2 * DN_WIDTH + (h + 1) * 128])
        q_st, k_st, v_st = stack(qs), stack(ks), stack(vs)
        b_st = stack([beta_all[:, h:h + 1] for h in heads])
        gc = stack([gc_all[:, 8 + h:9 + h] for h in heads])
        gc_row = jnp.concatenate([gc_t[8 + h:9 + h, :] for h in heads], axis=1)
        g_last = stack([jnp.broadcast_to(gc_all[C - 1:C, 8 + h:9 + h], (C, 1)) for h in heads])
        eg = jnp.exp(gc)
        decay = jnp.where(tril, jnp.exp(jnp.where(tril, gc - gc_row, 0.0)), 0.0)
        kb = k_st * b_st
        k_b = k_st.astype(BF16)
        a_mat = jnp.where(stril, _dot_nt(kb.astype(BF16), k_b) * decay, 0.0)
        pw = -a_mat
        t_mat = eye + pw
        for _ in range(max(int(math.log2(C)) - 1, 0)):
            pw_b = pw.astype(BF16)
            pw = _dot(pw_b, pw_b)
            t_mat = t_mat + _dot(t_mat.astype(BF16), pw.astype(BF16))
        t_b = t_mat.astype(BF16)
        u = _dot(t_b, (v_st * b_st).astype(BF16))
        w_b = _dot(t_b, (kb * eg).astype(BF16)).astype(BF16)
        qk = jnp.where(tril, _dot_nt(q_st.astype(BF16), k_b) * decay, 0.0)
        qe_b = (q_st * eg).astype(BF16)
        k_dec = k_st * jnp.exp(g_last - gc)
        s_b = [S[h].astype(BF16) for h in heads]
        rows = lambda a, n: a[n * C:(n + 1) * C]
        v_new = u - stack([_dot(rows(w_b, n), s_b[n]) for n in range(DN_GROUP)])
        v_b = v_new.astype(BF16)
        o_st = stack([_dot(rows(qe_b, n), s_b[n]) for n in range(DN_GROUP)]) + _dot(qk.astype(BF16), v_b)
        for n, h in enumerate(heads):
            S[h] = (S[h] * jnp.exp(gc_all[C - 1:C, 8 + h:9 + h])
                    + _dot(rows(k_dec, n).T.astype(BF16), rows(v_b, n)))
            outs.append(_rms_rows(rows(o_st, n), og_ref[...]))
    o_ref[0] = jnp.concatenate(outs, axis=-1)

    @pl.when(i == n_i - 1)
    def _():
        st_ref[0] = S[...]


def _gdn_core(qkv, ba, h0, s0, cw, alog, dt, og, *, C, t_valid):
    B, T, _ = qkv.shape
    per_b = lambda shape: pl.BlockSpec((1,) + shape, lambda b, i: (b,) + (0,) * len(shape))
    tile = lambda n: pl.BlockSpec((1, C, n), lambda b, i: (b, i, 0))
    return pl.pallas_call(
        functools.partial(_gdn_body, C=C, t_valid=t_valid),
        grid=(B, T // C),
        in_specs=[tile(3 * DN_WIDTH), tile(128), per_b((8, 3 * DN_WIDTH)),
                  per_b((DN_HEADS, 128, 128)), _const_spec((8, 3 * DN_WIDTH)),
                  _const_spec((1, 128)), _const_spec((1, 128)), _const_spec((1, 128))],
        out_specs=[tile(DN_WIDTH), per_b((DN_HEADS, 128, 128))],
        out_shape=[jax.ShapeDtypeStruct((B, T, DN_WIDTH), F32),
                   jax.ShapeDtypeStruct((B, DN_HEADS, 128, 128), F32)],
        scratch_shapes=[pltpu.VMEM((8 + C, 3 * DN_WIDTH), F32),
                        pltpu.VMEM((DN_HEADS, 128, 128), F32)],
        compiler_params=_cp("parallel", "arbitrary"),
        name="gdn_core",
    )(qkv, ba, h0, s0, cw, alog, dt, og)


CONF_HIST = 32


def _conf_body(u_ref, h0_ref, cw_ref, cb_ref, lg_ref, lb_ref, c_ref, buf, *, tm, carry):
    i = pl.program_id(1)

    @pl.when(i == 0)
    def _():
        buf[0:CONF_HIST, :] = h0_ref[0]

    buf[CONF_HIST:CONF_HIST + tm, :] = u_ref[0]
    acc = jnp.zeros((tm, 1024), F32)
    for j in range(CONF_KERNEL):
        acc = acc + cw_ref[j:j + 1, :] * buf[pl.ds(CONF_HIST - (CONF_KERNEL - 1) + j, tm), :]
    c = acc + cb_ref[...]
    mu = jnp.mean(c, axis=-1, keepdims=True)
    var = jnp.mean(jnp.square(c - mu), axis=-1, keepdims=True)
    c = (c - mu) * lax.rsqrt(var + NORM_EPS) * lg_ref[...] + lb_ref[...]
    c_ref[0] = _silu(c)
    if carry:
        buf[0:CONF_HIST, :] = buf[tm:tm + CONF_HIST, :]


def _conf_conv(u, h0, cw, cb, lg, lb):
    B, T, _ = u.shape
    tm = min(256, T)
    return pl.pallas_call(
        functools.partial(_conf_body, tm=tm, carry=T > tm),
        grid=(B, T // tm),
        in_specs=[pl.BlockSpec((1, tm, 1024), lambda b, i: (b, i, 0)),
                  pl.BlockSpec((1, CONF_HIST, 1024), lambda b, i: (b, 0, 0)),
                  _const_spec((32, 1024)), _const_spec((1, 1024)), _const_spec((1, 1024)),
                  _const_spec((1, 1024))],
        out_specs=pl.BlockSpec((1, tm, 1024), lambda b, i: (b, i, 0)),
        out_shape=jax.ShapeDtypeStruct((B, T, 1024), F32),
        scratch_shapes=[pltpu.VMEM((CONF_HIST + tm, 1024), F32)],
        compiler_params=_cp("parallel", "arbitrary"),
        name="conf_conv",
    )(u, h0, cw, cb, lg, lb)


def _block_diag_ones(n, seg):
    r = jnp.arange(n) // seg
    return (r[:, None] == r[None, :]).astype(BF16)


def _nsa_weights(w_in, q_gain, k_gain, cmp_pos, cmp_w, gate_b, w_out):
    q_w, kvc_w, kvs_w, kvw_w, gl_w, z_w = jnp.split(
        w_in, [1024, 1536, 2048, 2560, 2608], axis=1)
    wcat = jnp.concatenate([q_w, kvc_w, kvs_w, kvw_w, z_w, jnp.pad(gl_w, ((0, 0), (0, 80)))],
                           axis=1).astype(BF16)
    w5 = cmp_w.reshape(2, 2, CMP_STRIDE, A_HEAD_DIM, A_HEAD_DIM)
    eye_g = jnp.eye(A_KV_GROUPS, dtype=F32)
    wbig = jnp.einsum('chlde,gb->lcbdhge', w5, eye_g).reshape(CMP_STRIDE, 2, 256, 512).astype(BF16)
    pe4 = cmp_pos.reshape(2, 2, CMP_STRIDE, A_HEAD_DIM)
    pe = jnp.broadcast_to(pe4.transpose(1, 2, 0, 3)[:, :, :, None, :],
                          (2, CMP_STRIDE, 2, A_KV_GROUPS, A_HEAD_DIM)).reshape(2 * CMP_STRIDE, 512)
    pe = jnp.pad(pe, ((0, PAGE_SIZE - 2 * CMP_STRIDE), (0, 0)))
    heads = jnp.arange(1024) // A_HEAD_DIM
    ex = jnp.concatenate(
        [(jnp.arange(128)[:, None] == (br * A_HEADS + heads)[None, :]) for br in range(3)],
        axis=1).astype(BF16)
    return dict(
        wcat=wcat, wbig=wbig, pe=pe, ex=ex,
        bd=_block_diag_ones(1024, A_HEAD_DIM),
        qg=(jnp.tile(q_gain, A_HEADS) * (A_HEAD_DIM ** -0.5))[None, :],
        kcg=jnp.tile(k_gain[0], A_KV_GROUPS)[None, :],
        ksg=jnp.tile(k_gain[1], A_KV_GROUPS)[None, :],
        kwg=jnp.tile(k_gain[2], A_KV_GROUPS)[None, :],
        gb=jnp.pad(gate_b, (0, 80))[None, :],
        w_out=w_out.astype(BF16))


def _split3(x):
    p1 = x.astype(BF16)
    p2 = (x - p1.astype(F32)).astype(BF16)
    p3 = (x - p1.astype(F32) - p2.astype(F32)).astype(BF16)
    return p1, p2, p3


def _alibi_query_feats():
    slope = jnp.exp2(-8.0 * jnp.arange(1, A_HEADS + 1, dtype=F32) / A_HEADS)
    s1, s2, s3 = _split3(slope)
    sc = jnp.asarray(SLC_BLOCK, BF16)
    f = jnp.stack([s1 * sc, s2 * sc, s3 * sc, s1, s2, s3], axis=-1)
    return jnp.pad(f, ((0, 0), (0, 58))).reshape(A_KV_GROUPS, A_GROUP_SIZE, 64)


def _key_pos_feats(T):
    kp = jnp.arange(T, dtype=jnp.int32)
    a = (kp // SLC_BLOCK).astype(BF16)
    b = (kp % SLC_BLOCK).astype(BF16)
    return jnp.pad(jnp.stack([a, a, a, b, b, b], axis=-1), ((0, 0), (0, 58)))


def _feature_major(cache):
    L, n, rows = cache.shape[:3]
    return cache.transpose(0, 1, 3, 4, 5, 2).reshape(L, n, A_KV_WIDTH, rows)


def _nsa_layer(y, g_norm, wts, B, T, *, past=None):
    M = B * T
    x2 = y.reshape(M, D_MODEL)
    proj_args = (x2, g_norm, wts['wcat'], wts['bd'], wts['qg'], wts['ksg'], wts['kwg'], wts['gb'])
    bd4 = wts['bd'][:256, :256]
    if past is None:
        alibi = _alibi_query_feats().reshape(A_HEADS, 64).astype(F32)
        onehot = (jnp.arange(T)[:, None] // SLC_BLOCK == jnp.arange(N_BLK)[None, :]).astype(F32)
        pos_feats = jnp.concatenate([_key_pos_feats(T).astype(F32), onehot], axis=-1)
        qn, kvc, kvs, kvw, sz, gt, qa, kxs, vts, kxw, vtw = _nsa_proj(
            *proj_args, alibi=alibi, pos_feats=pos_feats, seq_len=T)
        n_pg = T // PAGE_SIZE
        table = jnp.arange(B * n_pg, dtype=jnp.int32).reshape(B, n_pg)
        proj, bias = _cmp_proj(kvc.reshape(B * n_pg, PAGE_SIZE, A_KV_WIDTH), table,
                               wts['wbig'], wts['pe'])
        kc, vc = _cmp_fin(proj, bias, bd4, wts['kcg'])
        oc, selb, flags = _cmp_attn_t(qa, kc, vc)
        flags = (flags[:, :, :, 0, :N_KTILE] > 0).astype(jnp.int32).reshape(-1)
        os_, ow = _p_attn(flags, qa, selb, kxs, vts, kxw, vtw)
        oc, os_, ow = (o.reshape(M, 1024) for o in (oc, os_, ow))
        new_win = kvw.reshape(B, T, A_KV_WIDTH)[:, T - min(WINDOW, T):]
    else:
        cmp_t, slc_t, win_t, win_buf, layer, table, past_len = past
        qn, kvc, kvs, kvw, sz, gt = _nsa_proj(*proj_args)
        q5 = qn.reshape(B, T, A_KV_GROUPS, A_GROUP_SIZE, A_HEAD_DIM).transpose(0, 2, 3, 1, 4)
        proj, bias = _cmp_proj(cmp_t, table, wts['wbig'], wts['pe'], layer=layer)
        kc, vc = _cmp_fin(proj, bias, bd4, wts['kcg'])
        q5p = jnp.pad(q5, ((0, 0), (0, 0), (0, 0), (0, SQ - T), (0, 0)))
        qe = jnp.pad(q5p, ((0, 0),) * 4 + ((0, 64),))
        oc, sel = _cmp_attn(qe, kc, vc, qb=SQ, t0=past_len, n_pick=N_SELECT - 1)
        eye_g = jnp.eye(A_KV_GROUPS, dtype=BF16)
        qx = jnp.einsum('bgrqd,gh->bgrqhd', q5p, eye_g).reshape(B, S_ROWS, 256)
        new_t = lambda a: jnp.pad(a.reshape(B, T, A_KV_WIDTH).transpose(0, 2, 1),
                                  ((0, 0), (0, 0), (0, PAGE_SIZE - T)))
        os_, ow = _s_attn(table, slc_t, qx, sel, new_t(kvs), win_t, new_t(kvw),
                          layer=layer, past_len=past_len)
        oc, os_, ow = (o[:, :T].reshape(M, 1024) for o in (oc, os_, ow))
        w_rows = win_buf.reshape(B, -1, A_KV_WIDTH)
        full_w = jnp.concatenate([w_rows, kvw.reshape(B, T, A_KV_WIDTH)], axis=1)
        new_win = full_w[:, full_w.shape[1] - w_rows.shape[1]:]
    y_new = _nsa_out(x2, oc, os_, ow, gt, sz, wts['ex'], wts['w_out']).reshape(B, T, D_MODEL)
    shape6 = lambda a: a.reshape(B, -1, 2, A_KV_GROUPS, A_HEAD_DIM)
    return y_new, shape6(kvc), shape6(kvs), shape6(new_win)


def _gdn_layer(y, g_norm, w_in, conv_w, a_log, dt_bias, o_gain, w_out, B, T, conv_buf, s0):
    M = B * T
    x2 = y.reshape(M, D_MODEL)
    wcat = jnp.pad(w_in, ((0, 0), (0, 112))).astype(BF16)
    qkv, sz, ba = _gdn_proj(x2, g_norm, wcat)
    qkv3 = qkv.reshape(B, T, 3 * DN_WIDTH)
    C = DN_CHUNK if T >= DN_CHUNK else 8
    Tp = -(-T // C) * C
    pad_t = lambda a: jnp.pad(a, ((0, 0), (0, Tp - T), (0, 0)))
    h0 = jnp.pad(conv_buf, ((0, 0), (8 - (DN_CONV - 1), 0), (0, 0)))
    lane8 = lambda v: jnp.pad(v, (8, 112))[None, :]
    o, s_new = _gdn_core(pad_t(qkv3), pad_t(ba.reshape(B, T, 128)), h0, s0,
                         jnp.pad(conv_w, ((0, 8 - DN_CONV), (0, 0))), lane8(a_log), lane8(dt_bias),
                         o_gain[None, :], C=C, t_valid=min(T, C) if Tp != T else C)
    y_new = _out_proj(x2, o[:, :T].reshape(M, DN_WIDTH), sz, w_out.astype(BF16))
    x_ext_tail = jnp.concatenate([conv_buf, qkv3], axis=1)[:, T:]
    return y_new.reshape(B, T, D_MODEL), s_new, x_ext_tail


def _conf_layer(y, g_norm, w_in, conv_w, conv_b, ln_g, ln_b, w_out, B, T, conv_buf):
    M = B * T
    x2 = y.reshape(M, D_MODEL)
    u, sz = _conf_proj(x2, g_norm, w_in.astype(BF16))
    u3 = u.reshape(B, T, 1024)
    h0 = jnp.pad(conv_buf, ((0, 0), (CONF_HIST - (CONF_KERNEL - 1), 0), (0, 0)))
    c = _conf_conv(u3, h0, jnp.pad(conv_w, ((0, 32 - CONF_KERNEL), (0, 0))), conv_b[None, :],
                   ln_g[None, :], ln_b[None, :])
    y_new = _out_proj(x2, c.reshape(M, 1024), sz, w_out.astype(BF16))
    x_ext_tail = jnp.concatenate([conv_buf, u3], axis=1)[:, T:]
    return y_new.reshape(B, T, D_MODEL), x_ext_tail


def kernel(x_prompt, x_sample, cache_cmp_kv, cache_slc_kv, cache_win_kv, state_delta, state_delta_conv, state_conv, page_table, norm_g, a_w_in, a_q_gain, a_k_gain, a_cmp_pos, a_cmp_w, a_gate_b, a_w_out, b_w_in, b_conv_w, b_a_log, b_dt_bias, b_o_gain, b_w_out, c_w_in, c_conv_w, c_conv_b, c_ln_g, c_ln_b, c_w_out):
    depth = norm_g.shape[0]
    bp, tp, _ = x_prompt.shape
    bs, ts, _ = x_sample.shape
    past_len = page_table.shape[1] * PAGE_SIZE
    yp, ys = x_prompt, x_sample
    cmp_t, slc_t, win_t = (_feature_major(c) for c in (cache_cmp_kv, cache_slc_kv, cache_win_kv))
    outs = [[] for _ in range(12)]
    for i in range(depth):
        j, kind = divmod(i, N_MIXERS)
        gn = norm_g[i][None, :]
        if kind == 0:
            wts = _nsa_weights(a_w_in[j], a_q_gain[j], a_k_gain[j], a_cmp_pos[j], a_cmp_w[j],
                               a_gate_b[j], a_w_out[j])
            yp, cp_, sp_, wp_ = _nsa_layer(yp, gn, wts, bp, tp)
            ys, cs_, ss_, ws_ = _nsa_layer(
                ys, gn, wts, bs, ts,
                past=(cmp_t, slc_t, win_t, cache_win_kv[j], j, page_table, past_len))
            for lst, v in zip(outs[0:6], (cp_, cs_, sp_, ss_, wp_, ws_)):
                lst.append(v)
        elif kind == 1:
            wb = (b_w_in[j], b_conv_w[j], b_a_log[j], b_dt_bias[j], b_o_gain[j], b_w_out[j])
            yp, st_p, cv_p = _gdn_layer(yp, gn, *wb, bp, tp,
                                        jnp.zeros((bp, DN_CONV - 1, 3 * DN_WIDTH), F32),
                                        jnp.zeros((bp, DN_HEADS, DN_HEAD_DIM, DN_HEAD_DIM), F32))
            ys, st_s, cv_s = _gdn_layer(ys, gn, *wb, bs, ts, state_delta_conv[j], state_delta[j])
            for lst, v in zip(outs[6:10], (st_p, st_s, cv_p, cv_s)):
                lst.append(v)
        else:
            wc = (c_w_in[j], c_conv_w[j], c_conv_b[j], c_ln_g[j], c_ln_b[j], c_w_out[j])
            yp, cb_p = _conf_layer(yp, gn, *wc, bp, tp, jnp.zeros((bp, CONF_KERNEL - 1, 1024), F32))
            ys, cb_s = _conf_layer(ys, gn, *wc, bs, ts, state_conv[j])
            for lst, v in zip(outs[10:12], (cb_p, cb_s)):
                lst.append(v)
    return (yp, ys) + tuple(jnp.stack(lst) for lst in outs)
```
